```python
import math
import jax, jax.numpy as jnp
from jax import lax
import numpy as np

D_MODEL = 1024
BATCH = 8
SEQ = 2048
DEPTH = 4

N_MIXERS = 3
MEM_LEN = 256
DA_HEADS = 8
DA_HEAD_DIM = 64
DA_QBLOCK = 128
ROPE_THETA = 10000.0
POOL_WINDOWS = (2, 4, 8, 16)
POOL_GROUPS = len(POOL_WINDOWS)
POOL_GROUP_DIM = D_MODEL // POOL_GROUPS
RET_HEADS = 4
RET_QK_DIM = D_MODEL // RET_HEADS
RET_V_DIM = 2 * RET_QK_DIM
RET_CHUNK = 128
XA_HEADS = 4
XA_HEAD_DIM = D_MODEL // XA_HEADS
D_FF = 2816
DEEPNORM_ALPHA = (2 * DEPTH) ** 0.25
DEEPNORM_BETA = (8 * DEPTH) ** -0.25
LN_EPS = 1e-5

N_DA = len(range(0, DEPTH, N_MIXERS))
N_POOL = len(range(1, DEPTH, N_MIXERS))
N_RET = len(range(2, DEPTH, N_MIXERS))

kernel_name = "hybrid_diffattn_pool_retention_deepnorm"


def layer_norm(x, g, b):
    xf = x.astype(jnp.float32)
    mu = jnp.mean(xf, axis=-1, keepdims=True)
    var = jnp.mean(jnp.square(xf - mu), axis=-1, keepdims=True)
    y = (xf - mu) * lax.rsqrt(var + LN_EPS) * g.astype(jnp.float32) + b.astype(jnp.float32)
    return y.astype(x.dtype)


def rope(x, cos, sin):
    half = x.shape[-1] // 2
    xf = x.astype(jnp.float32)
    x1, x2 = xf[..., :half], xf[..., half:]
    return jnp.concatenate([x1 * cos - x2 * sin, x2 * cos + x1 * sin], axis=-1).astype(x.dtype)


def swiglu_ffn(x, w_in, w_out):
    h = x @ w_in
    g, u = jnp.split(h, 2, axis=-1)
    return (jax.nn.silu(g) * u) @ w_out


def diff_attention(x, w_qkv, w_o, lam_q, lam_k, subln_g, cos, sin, lam_init):
    B, S, _ = x.shape
    H, dk = DA_HEADS, DA_HEAD_DIM
    q, k, v = jnp.split(x @ w_qkv, 3, axis=-1)
    q = q.reshape(B, S, H, 2, dk)
    k = k.reshape(B, S, H, 2, dk)
    v = v.reshape(B, S, H, 2 * dk)
    c5, s5 = cos[:, :, None, None, :], sin[:, :, None, None, :]
    q = rope(q, c5, s5)
    k = rope(k, c5, s5)
    lq = lam_q.astype(jnp.float32)
    lk = lam_k.astype(jnp.float32)
    lam = jnp.exp(jnp.sum(lq[0] * lk[0])) - jnp.exp(jnp.sum(lq[1] * lk[1])) + lam_init
    scale = dk ** -0.5
    nb = S // DA_QBLOCK
    qb = q.reshape(B, nb, DA_QBLOCK, H, 2, dk).transpose(1, 0, 2, 3, 4, 5)
    kpos = jnp.arange(S)

    def block(args):
        qblk, i = args
        s = jnp.einsum('bqhcd,bkhcd->bhcqk', qblk, k).astype(jnp.float32) * scale
        qpos = i * DA_QBLOCK + jnp.arange(DA_QBLOCK)
        mask = kpos[None, :] <= qpos[:, None]
        s = jnp.where(mask, s, jnp.finfo(jnp.float32).min)
        p = jax.nn.softmax(s, axis=-1)
        a = p[:, :, 0] - lam * p[:, :, 1]
        return jnp.einsum('bhqk,bkhe->bqhe', a.astype(v.dtype), v)

    o = lax.map(block, (qb, jnp.arange(nb)))
    o = o.transpose(1, 0, 2, 3, 4).reshape(B, S, H, 2 * dk).astype(jnp.float32)
    o = o * lax.rsqrt(jnp.mean(jnp.square(o), axis=-1, keepdims=True) + LN_EPS)
    o = o * subln_g.astype(jnp.float32) * (1.0 - lam_init)
    return o.reshape(B, S, H * 2 * dk).astype(x.dtype) @ w_o


def pool_mixer(x, w_grp, b_grp, scale):
    B, S, D = x.shape
    xg = x.reshape(B, S, POOL_GROUPS, POOL_GROUP_DIM).astype(jnp.float32)
    cs = jnp.cumsum(xg, axis=1)
    t = jnp.arange(S)
    pooled = []
    for g, w in enumerate(POOL_WINDOWS):
        c = cs[:, :, g]
        prev = jnp.pad(c, ((0, 0), (w, 0), (0, 0)))[:, :S]
        cnt = jnp.minimum(t + 1, w).astype(jnp.float32)[None, :, None]
        pooled.append((c - prev) / cnt)
    pooled = jnp.stack(pooled, axis=2) - xg
    y = jnp.einsum('bsgc,gcd->bsgd', pooled.astype(x.dtype), w_grp) + b_grp
    return y.reshape(B, S, D) * scale


def retention(x, w_qkvg, w_o, cos, sin):
    B, S, _ = x.shape
    H, dk, dv, C = RET_HEADS, RET_QK_DIM, RET_V_DIM, RET_CHUNK
    proj = x @ w_qkvg
    q, k, v, g = jnp.split(proj, [H * dk, 2 * H * dk, 2 * H * dk + H * dv], axis=-1)
    c4, s4 = cos[:, :, None, :], sin[:, :, None, :]
    q = rope(q.reshape(B, S, H, dk), c4, s4).astype(jnp.float32)
    k = rope(k.reshape(B, S, H, dk), c4, s4).astype(jnp.float32) * (dk ** -0.5)
    v = v.reshape(B, S, H, dv).astype(jnp.float32)
    log_gamma = jnp.log(1.0 - jnp.exp2(-5.0 - jnp.arange(H, dtype=jnp.float32)))
    idx = jnp.arange(C, dtype=jnp.float32)
    rel = idx[:, None] - idx[None, :]
    d_intra = jnp.where(rel[None] >= 0,
                        jnp.exp(jnp.maximum(rel, 0.0)[None] * log_gamma[:, None, None]), 0.0)
    q_decay = jnp.exp((idx[:, None] + 1.0) * log_gamma[None, :])
    k_decay = jnp.exp((C - 1.0 - idx[:, None]) * log_gamma[None, :])
    chunk_decay = jnp.exp(C * log_gamma)
    nc = S // C

    def to_chunks(a):
        return a.reshape(B, nc, C, H, a.shape[-1]).transpose(1, 0, 2, 3, 4)

    def step(R, inp):
        qc, kc, vc = inp
        att = jnp.einsum('bihd,bjhd->bhij', qc, kc) * d_intra[None]
        inner = jnp.einsum('bhij,bjhe->bihe', att, vc)
        cross = jnp.einsum('bihd,bhde->bihe', qc, R) * q_decay[None, :, :, None]
        R = R * chunk_decay[None, :, None, None] + jnp.einsum(
            'bjhd,bjhe->bhde', kc * k_decay[None, :, :, None], vc)
        return R, inner + cross

    R0 = jnp.zeros((B, H, dk, dv), jnp.float32)
    _, o = lax.scan(step, R0, (to_chunks(q), to_chunks(k), to_chunks(v)))
    o = o.transpose(1, 0, 2, 3, 4).reshape(B, S, H, dv)
    mu = jnp.mean(o, axis=-1, keepdims=True)
    var = jnp.mean(jnp.square(o - mu), axis=-1, keepdims=True)
    o = ((o - mu) * lax.rsqrt(var + LN_EPS)).reshape(B, S, H * dv).astype(x.dtype)
    return (jax.nn.silu(g) * o) @ w_o


def memory_cross_attention(x, mem, wq, wkv, wo):
    B, S, _ = x.shape
    M = mem.shape[1]
    q = (x @ wq).reshape(B, S, XA_HEADS, XA_HEAD_DIM)
    k, v = jnp.split(mem @ wkv, 2, axis=-1)
    k = k.reshape(B, M, XA_HEADS, XA_HEAD_DIM)
    v = v.reshape(B, M, XA_HEADS, XA_HEAD_DIM)
    s = jnp.einsum('bshd,bmhd->bhsm', q, k).astype(jnp.float32) * (XA_HEAD_DIM ** -0.5)
    p = jax.nn.softmax(s, axis=-1).astype(v.dtype)
    o = jnp.einsum('bhsm,bmhd->bshd', p, v).reshape(B, S, D_MODEL)
    return o @ wo


def setup_inputs(seed: int = 0) -> dict:
    key = jax.random.key(seed)
    ks = jax.random.split(key, 24)
    f32 = jnp.float32
    D, F = D_MODEL, D_FF
    nrm = lambda k, shape, std: jax.random.normal(k, shape, f32) * std
    x = nrm(ks[0], (BATCH, SEQ, D), 1.0)
    mem = nrm(ks[1], (BATCH, MEM_LEN, D), 1.0)
    offset = jax.random.randint(ks[2], (BATCH, 1), 0, 4096, dtype=jnp.int32)
    positions = (jnp.arange(SEQ, dtype=jnp.int32)[None, :] + offset).astype(jnp.int32)
    ffn_w_in = nrm(ks[3], (DEPTH, 2, D, 2 * F), D ** -0.5)
    ffn_w_out = nrm(ks[4], (DEPTH, 2, F, D), F ** -0.5 * DEEPNORM_BETA)
    ln_g = 1.0 + nrm(ks[5], (DEPTH, 4, D), 0.02)
    ln_b = nrm(ks[6], (DEPTH, 4, D), 0.02)
    da_w_qkv = nrm(ks[7], (N_DA, D, 3 * DA_HEADS * 2 * DA_HEAD_DIM), D ** -0.5)
    da_w_o = nrm(ks[8], (N_DA, DA_HEADS * 2 * DA_HEAD_DIM, D),
                 (DA_HEADS * 2 * DA_HEAD_DIM) ** -0.5 * DEEPNORM_BETA)
    da_lam_q = nrm(ks[9], (N_DA, 2, DA_HEAD_DIM), 0.1)
    da_lam_k = nrm(ks[10], (N_DA, 2, DA_HEAD_DIM), 0.1)
    da_subln_g = 1.0 + nrm(ks[11], (N_DA, 2 * DA_HEAD_DIM), 0.02)
    pool_w = nrm(ks[12], (N_POOL, POOL_GROUPS, POOL_GROUP_DIM, POOL_GROUP_DIM),
                 POOL_GROUP_DIM ** -0.5 * DEEPNORM_BETA)
    pool_b = nrm(ks[13], (N_POOL, POOL_GROUPS, POOL_GROUP_DIM), 0.02)
    pool_scale = 1.0 + nrm(ks[14], (N_POOL, D), 0.02)
    ret_w_qkvg = nrm(ks[15], (N_RET, D, 2 * RET_HEADS * RET_QK_DIM + 2 * RET_HEADS * RET_V_DIM),
                     D ** -0.5)
    ret_w_o = nrm(ks[16], (N_RET, RET_HEADS * RET_V_DIM, D),
                  (RET_HEADS * RET_V_DIM) ** -0.5 * DEEPNORM_BETA)
    xa_wq = nrm(ks[17], (DEPTH, D, D), D ** -0.5)
    xa_wkv = nrm(ks[18], (DEPTH, D, 2 * D), D ** -0.5)
    xa_wo = nrm(ks[19], (DEPTH, D, D), D ** -0.5 * DEEPNORM_BETA)
    return {"x": x, "mem": mem, "positions": positions,
            "ffn_w_in": ffn_w_in, "ffn_w_out": ffn_w_out, "ln_g": ln_g, "ln_b": ln_b,
            "da_w_qkv": da_w_qkv, "da_w_o": da_w_o, "da_lam_q": da_lam_q,
            "da_lam_k": da_lam_k, "da_subln_g": da_subln_g,
            "pool_w": pool_w, "pool_b": pool_b, "pool_scale": pool_scale,
            "ret_w_qkvg": ret_w_qkvg, "ret_w_o": ret_w_o,
            "xa_wq": xa_wq, "xa_wkv": xa_wkv, "xa_wo": xa_wo}


def reference(x, mem, positions, ffn_w_in, ffn_w_out, ln_g, ln_b,
              da_w_qkv, da_w_o, da_lam_q, da_lam_k, da_subln_g,
              pool_w, pool_b, pool_scale, ret_w_qkvg, ret_w_o,
              xa_wq, xa_wkv, xa_wo):
    pos = positions.astype(jnp.float32)[..., None]
    da_inv = 1.0 / (ROPE_THETA ** (jnp.arange(0, DA_HEAD_DIM, 2, dtype=jnp.float32) / DA_HEAD_DIM))
    da_ang = pos * da_inv[None, None, :]
    da_cos, da_sin = jnp.cos(da_ang), jnp.sin(da_ang)
    ret_inv = 1.0 / (ROPE_THETA ** jnp.linspace(0.0, 1.0, RET_QK_DIM // 2, dtype=jnp.float32))
    ret_ang = pos * ret_inv[None, None, :]
    ret_cos, ret_sin = jnp.cos(ret_ang), jnp.sin(ret_ang)
    a = DEEPNORM_ALPHA
    for i in range(DEPTH):
        m, j = i % N_MIXERS, i // N_MIXERS
        x = layer_norm(a * x + 0.5 * swiglu_ffn(x, ffn_w_in[i, 0], ffn_w_out[i, 0]),
                       ln_g[i, 0], ln_b[i, 0])
        if m == 0:
            lam_init = 0.8 - 0.6 * math.exp(-0.3 * i)
            h = diff_attention(x, da_w_qkv[j], da_w_o[j], da_lam_q[j], da_lam_k[j],
                               da_subln_g[j], da_cos, da_sin, lam_init)
        elif m == 1:
            h = pool_mixer(x, pool_w[j], pool_b[j], pool_scale[j])
        else:
            h = retention(x, ret_w_qkvg[j], ret_w_o[j], ret_cos, ret_sin)
        x = layer_norm(a * x + h, ln_g[i, 1], ln_b[i, 1])
        x = layer_norm(a * x + memory_cross_attention(x, mem, xa_wq[i], xa_wkv[i], xa_wo[i]),
                       ln_g[i, 2], ln_b[i, 2])
        x = layer_norm(a * x + 0.5 * swiglu_ffn(x, ffn_w_in[i, 1], ffn_w_out[i, 1]),
                       ln_g[i, 3], ln_b[i, 3])
    return x
```

```python
import functools
import math

import jax
import jax.numpy as jnp
import numpy as np
from jax import lax
from jax.experimental import pallas as pl
from jax.experimental.pallas import tpu as pltpu

D_MODEL = 1024
DEPTH = 4
N_MIXERS = 3
DA_HEADS = 8
DA_HEAD_DIM = 64
DA_V_DIM = 2 * DA_HEAD_DIM
ROPE_THETA = 10000.0
POOL_WINDOWS = (2, 4, 8, 16)
POOL_GROUP_DIM = D_MODEL // len(POOL_WINDOWS)
POOL_HALO = 16
RET_HEADS = 4
RET_QK_DIM = D_MODEL // RET_HEADS
RET_V_DIM = 2 * RET_QK_DIM
RET_CHUNK = 128
XA_HEADS = 4
XA_HEAD_DIM = D_MODEL // XA_HEADS
D_FF = 2816
ALPHA = (2 * DEPTH) ** 0.25
LN_EPS = 1e-5

BF16 = jnp.bfloat16
F32 = jnp.float32

V7X_VMEM_LIMIT_BYTES = 56 * 1024 * 1024
LANES = 128


def _params(n_axes):
    return pltpu.CompilerParams(
        dimension_semantics=("arbitrary",) * n_axes,
        vmem_limit_bytes=V7X_VMEM_LIMIT_BYTES)


def _resident(shape):
    nd = len(shape)
    return pl.BlockSpec(shape, lambda *_: (0,) * nd, pipeline_mode=pl.Buffered(1))


def _layer_norm(y, g, b):
    mu = jnp.mean(y, axis=-1, keepdims=True)
    d = y - mu
    var = jnp.mean(d * d, axis=-1, keepdims=True)
    return d * lax.rsqrt(var + LN_EPS) * g + b


def _dot(a, b):
    return jnp.dot(a, b, preferred_element_type=F32)


def _dot_nt(a, b):
    return lax.dot_general(a, b, (((1,), (1,)), ((), ())), preferred_element_type=F32)


def _dot_tn(a, b):
    return lax.dot_general(a, b, (((0,), (0,)), ((), ())), preferred_element_type=F32)


def _matmul_kernel(a_ref, w_ref, o_ref):
    o_ref[0] = _dot(a_ref[...].astype(BF16), w_ref[0]).astype(o_ref.dtype)


def _mem_kv(mem2d, wkv):
    R = mem2d.shape[0]
    L, _, N = wkv.shape
    return pl.pallas_call(
        _matmul_kernel,
        grid=(L,),
        in_specs=[pl.BlockSpec((R, D_MODEL), lambda l: (0, 0)),
                  pl.BlockSpec((1, D_MODEL, N), lambda l: (l, 0, 0))],
        out_specs=pl.BlockSpec((1, R, N), lambda l: (l, 0, 0)),
        out_shape=jax.ShapeDtypeStruct((L, R, N), BF16),
        compiler_params=_params(1),
        name="mem_kv",
    )(mem2d, wkv)


FFN_ROWS = 512
FFN_CHUNK = 512


def _ffn_ln_kernel(x_ref, win_ref, wout_ref, g_ref, b_ref, o_ref):
    x = x_ref[...]
    xb = x.astype(BF16)
    acc = jnp.zeros(x.shape, F32)
    for c0 in range(0, D_FF, FFN_CHUNK):
        c1 = min(c0 + FFN_CHUNK, D_FF)
        hg = _dot(xb, win_ref[:, c0:c1])
        hu = _dot(xb, win_ref[:, D_FF + c0:D_FF + c1])
        act = (hg * jax.nn.sigmoid(hg) * hu).astype(BF16)
        acc = acc + _dot(act, wout_ref[c0:c1, :])
    o_ref[...] = _layer_norm(ALPHA * x + 0.5 * acc, g_ref[...], b_ref[...])


def _ffn_ln(x2d, w_in, w_out, g, b):
    T = x2d.shape[0]
    row = pl.BlockSpec((FFN_ROWS, D_MODEL), lambda i: (i, 0))
    return pl.pallas_call(
        _ffn_ln_kernel,
        grid=(T // FFN_ROWS,),
        in_specs=[row, _resident((D_MODEL, 2 * D_FF)), _resident((D_FF, D_MODEL)),
                  _resident((1, D_MODEL)), _resident((1, D_MODEL))],
        out_specs=row,
        out_shape=jax.ShapeDtypeStruct((T, D_MODEL), F32),
        compiler_params=_params(1),
        name="ffn_ln",
    )(x2d, w_in, w_out, g, b)


PROJ_ROWS = 512


def _proj_ln_kernel(a_ref, w_ref, x_ref, g_ref, b_ref, o_ref):
    h = _dot(a_ref[...], w_ref[...])
    o_ref[...] = _layer_norm(ALPHA * x_ref[...] + h, g_ref[...], b_ref[...])


def _proj_ln(a2d, w, x2d, g, b):
    T, K = a2d.shape
    return pl.pallas_call(
        _proj_ln_kernel,
        grid=(T // PROJ_ROWS,),
        in_specs=[pl.BlockSpec((PROJ_ROWS, K), lambda i: (i, 0)),
                  _resident((K, D_MODEL)),
                  pl.BlockSpec((PROJ_ROWS, D_MODEL), lambda i: (i, 0)),
                  _resident((1, D_MODEL)), _resident((1, D_MODEL))],
        out_specs=pl.BlockSpec((PROJ_ROWS, D_MODEL), lambda i: (i, 0)),
        out_shape=jax.ShapeDtypeStruct((T, D_MODEL), F32),
        compiler_params=_params(1),
        name="proj_ln",
    )(a2d, w, x2d, g, b)


XA_ROWS = 512


def _xattn_ln_kernel(x_ref, kv_ref, wq_ref, wo_ref, g_ref, b_ref, o_ref):
    x = x_ref[0]
    q = (_dot(x.astype(BF16), wq_ref[...]) * (XA_HEAD_DIM ** -0.5)).astype(BF16)
    heads = []
    for h in range(XA_HEADS):
        lo, hi = h * XA_HEAD_DIM, (h + 1) * XA_HEAD_DIM
        s = _dot_nt(q[:, lo:hi], kv_ref[0, 0, :, lo:hi])
        e = jnp.exp(s - jnp.max(s, axis=-1, keepdims=True))
        p = e * (1.0 / jnp.sum(e, axis=-1, keepdims=True))
        heads.append(_dot(p.astype(BF16), kv_ref[0, 0, :, D_MODEL + lo:D_MODEL + hi]).astype(BF16))
    o = jnp.concatenate(heads, axis=-1)
    h_out = _dot(o, wo_ref[...])
    o_ref[0] = _layer_norm(ALPHA * x + h_out, g_ref[...], b_ref[...])


def _xattn_ln(x, kv_all, layer, wq, wo, g, b):
    B, S, _ = x.shape
    M = kv_all.shape[2]
    row = pl.BlockSpec((1, XA_ROWS, D_MODEL), lambda bi, i: (bi, i, 0))
    return pl.pallas_call(
        _xattn_ln_kernel,
        grid=(B, S // XA_ROWS),
        in_specs=[row,
                  pl.BlockSpec((1, 1, M, 2 * D_MODEL), lambda bi, i: (layer, bi, 0, 0)),
                  _resident((D_MODEL, D_MODEL)), _resident((D_MODEL, D_MODEL)),
                  _resident((1, D_MODEL)), _resident((1, D_MODEL))],
        out_specs=row,
        out_shape=jax.ShapeDtypeStruct((B, S, D_MODEL), F32),
        compiler_params=_params(2),
        name="xattn_ln",
    )(x, kv_all, wq, wo, g, b)


DA_PROJ_ROWS = 512
DA_QBLOCK = 256
DA_WIDTH = DA_HEADS * DA_V_DIM

_half = DA_HEAD_DIM // 2
_DA_HEAD_PERM = np.concatenate([np.arange(0, _half), np.arange(2 * _half, 3 * _half),
                                np.arange(_half, 2 * _half), np.arange(3 * _half, 4 * _half)])
_DA_QK_PERM = (np.arange(DA_HEADS)[:, None] * DA_V_DIM + _DA_HEAD_PERM[None, :]).reshape(-1)
_DA_COL_PERM = np.concatenate([_DA_QK_PERM, DA_WIDTH + _DA_QK_PERM,
                               2 * DA_WIDTH + np.arange(DA_WIDTH)])


def _da_proj_kernel(x_ref, w_ref, cos_ref, sin_ref, q_ref, k_ref, v_ref):
    xb = x_ref[0].astype(BF16)
    cos = cos_ref[0]
    sin = sin_ref[0]
    for part, out_ref, scale in ((0, q_ref, DA_HEAD_DIM ** -0.5), (1, k_ref, 1.0)):
        y = _dot(xb, w_ref[:, part * DA_WIDTH:(part + 1) * DA_WIDTH])
        for h in range(DA_HEADS):
            yh = y[:, h * DA_V_DIM:(h + 1) * DA_V_DIM]
            r = yh * cos + pltpu.roll(yh, DA_V_DIM // 2, axis=1) * sin
            out_ref[0, :, h * DA_V_DIM:(h + 1) * DA_V_DIM] = (r * scale).astype(BF16)
    v_ref[0] = _dot(xb, w_ref[:, 2 * DA_WIDTH:]).astype(BF16)


def _da_proj(x, w_qkv, cos_t, sin_t):
    B, S, _ = x.shape
    row = pl.BlockSpec((1, DA_PROJ_ROWS, D_MODEL), lambda bi, i: (bi, i, 0))
    tab = pl.BlockSpec((1, DA_PROJ_ROWS, LANES), lambda bi, i: (bi, i, 0))
    out = jax.ShapeDtypeStruct((B, S, DA_WIDTH), BF16)
    return pl.pallas_call(
        _da_proj_kernel,
        grid=(B, S // DA_PROJ_ROWS),
        in_specs=[row, _resident((D_MODEL, 3 * DA_WIDTH)), tab, tab],
        out_specs=[row, row, row],
        out_shape=[out, out, out],
        compiler_params=_params(2),
        name="da_proj",
    )(x, w_qkv, cos_t, sin_t)


def _da_attn_kernel(q_ref, k_ref, v_ref, lq_ref, lk_ref, sg_ref, o_ref, *, lam_init):
    S = q_ref.shape[1]
    tq = DA_QBLOCK
    lqk = lq_ref[...] * lk_ref[...]
    lam = (jnp.exp(jnp.sum(lqk[0:1], axis=-1, keepdims=True))
           - jnp.exp(jnp.sum(lqk[1:2], axis=-1, keepdims=True)) + lam_init)
    lane = lax.broadcasted_iota(jnp.int32, (1, DA_V_DIM), 1)
    map0 = ((lane // (DA_HEAD_DIM // 2)) % 2 == 0)
    sub_g = sg_ref[...] * (1.0 - lam_init)
    for i in range(S // tq):
        n = (i + 1) * tq
        q = q_ref[0, i * tq:(i + 1) * tq, :]
        zero = jnp.zeros_like(q)
        q2 = jnp.concatenate([jnp.where(map0, q, zero), jnp.where(map0, zero, q)], axis=0)
        s = _dot_nt(q2, k_ref[0, :n, :])
        row = lax.broadcasted_iota(jnp.int32, (2 * tq, n), 0) % tq + i * tq
        col = lax.broadcasted_iota(jnp.int32, (2 * tq, n), 1)
        s = jnp.where(col <= row, s, jnp.finfo(F32).min)
        e = jnp.exp(s - jnp.max(s, axis=-1, keepdims=True))
        inv = 1.0 / jnp.sum(e, axis=-1, keepdims=True)
        a = e[:tq] * inv[:tq] - e[tq:] * (lam * inv[tq:])
        o = _dot(a.astype(BF16), v_ref[0, :n, :])
        o = o * lax.rsqrt(jnp.mean(o * o, axis=-1, keepdims=True) + LN_EPS) * sub_g
        o_ref[0, i * tq:(i + 1) * tq, :] = o.astype(BF16)


def _da_attn(q, k, v, lam_q, lam_k, subln_g, lam_init):
    B, S, _ = q.shape
    head = pl.BlockSpec((1, S, DA_V_DIM), lambda bi, h: (bi, 0, h))
    return pl.pallas_call(
        functools.partial(_da_attn_kernel, lam_init=lam_init),
        grid=(B, DA_HEADS),
        in_specs=[head, head, head, _resident((2, DA_HEAD_DIM)), _resident((2, DA_HEAD_DIM)),
                  _resident((1, DA_V_DIM))],
        out_specs=head,
        out_shape=jax.ShapeDtypeStruct((B, S, DA_WIDTH), BF16),
        compiler_params=_params(2),
        name="da_attn",
    )(q, k, v, lam_q, lam_k, subln_g)


POOL_ROWS = 512


def _pool_ln_kernel(x_ref, prev_ref, w_ref, bias_ref, scale_ref, g_ref, b_ref, o_ref, buf_ref,
                    *, tiles_per_seq):
    i = pl.program_id(0)
    t0 = (i % tiles_per_seq) * POOL_ROWS
    x = x_ref[...]
    first = (i % tiles_per_seq) == 0
    buf_ref[0:POOL_HALO, :] = jnp.where(first, 0.0, prev_ref[...])
    buf_ref[POOL_HALO:, :] = x
    t = t0 + lax.broadcasted_iota(jnp.int32, (POOL_ROWS, 1), 0)
    outs = []
    for gi, w in enumerate(POOL_WINDOWS):
        lo, hi = gi * POOL_GROUP_DIM, (gi + 1) * POOL_GROUP_DIM
        xg = x[:, lo:hi]
        tot = xg
        for k in range(1, w):
            tot = tot + buf_ref[POOL_HALO - k:POOL_HALO - k + POOL_ROWS, lo:hi]
        cnt = jnp.minimum(t + 1, w).astype(F32)
        pooled = tot / cnt - xg
        outs.append(_dot(pooled.astype(BF16), w_ref[gi]))
    y = (jnp.concatenate(outs, axis=-1) + bias_ref[...]) * scale_ref[...]
    o_ref[...] = _layer_norm(ALPHA * x + y, g_ref[...], b_ref[...])


def _pool_ln(x2d, seq_len, w_grp, bias, scale, g, b):
    T = x2d.shape[0]
    tiles_per_seq = seq_len // POOL_ROWS
    halo_blocks = POOL_ROWS // POOL_HALO
    return pl.pallas_call(
        functools.partial(_pool_ln_kernel, tiles_per_seq=tiles_per_seq),
        grid=(T // POOL_ROWS,),
        in_specs=[pl.BlockSpec((POOL_ROWS, D_MODEL), lambda i: (i, 0)),
                  pl.BlockSpec((POOL_HALO, D_MODEL),
                               lambda i: (jnp.maximum(i * halo_blocks - 1, 0), 0)),
                  _resident(w_grp.shape), _resident((1, D_MODEL)), _resident((1, D_MODEL)),
                  _resident((1, D_MODEL)), _resident((1, D_MODEL))],
        out_specs=pl.BlockSpec((POOL_ROWS, D_MODEL), lambda i: (i, 0)),
        out_shape=jax.ShapeDtypeStruct((T, D_MODEL), F32),
        scratch_shapes=[pltpu.VMEM((POOL_HALO + POOL_ROWS, D_MODEL), F32)],
        compiler_params=_params(1),
        name="pool_ln",
    )(x2d, x2d, w_grp, bias, scale, g, b)


RET_PROJ_ROWS = 512
RET_QK_WIDTH = RET_HEADS * RET_QK_DIM
RET_V_WIDTH = RET_HEADS * RET_V_DIM
RET_STEP_ROWS = 512


def _ret_proj_kernel(x_ref, w_ref, cos_ref, sin_ref, q_ref, k_ref, v_ref, sg_ref):
    xb = x_ref[0].astype(BF16)
    cos = cos_ref[0]
    sin = sin_ref[0]
    half = RET_QK_DIM // 2
    for part, out_ref, scale in ((0, q_ref, 1.0), (1, k_ref, RET_QK_DIM ** -0.5)):
        y = _dot(xb, w_ref[:, part * RET_QK_WIDTH:(part + 1) * RET_QK_WIDTH])
        for h in range(RET_HEADS):
            x1 = y[:, h * RET_QK_DIM:h * RET_QK_DIM + half]
            x2 = y[:, h * RET_QK_DIM + half:(h + 1) * RET_QK_DIM]
            out_ref[0, :, h * RET_QK_DIM:h * RET_QK_DIM + half] = (
                (x1 * cos - x2 * sin) * scale).astype(BF16)
            out_ref[0, :, h * RET_QK_DIM + half:(h + 1) * RET_QK_DIM] = (
                (x2 * cos + x1 * sin) * scale).astype(BF16)
    v0 = 2 * RET_QK_WIDTH
    g0 = v0 + RET_V_WIDTH
    for c in range(0, RET_V_WIDTH, 1024):
        v_ref[0, :, c:c + 1024] = _dot(xb, w_ref[:, v0 + c:v0 + c + 1024]).astype(BF16)
        gate = _dot(xb, w_ref[:, g0 + c:g0 + c + 1024])
        sg_ref[0, :, c:c + 1024] = (gate * jax.nn.sigmoid(gate)).astype(BF16)


def _ret_proj(x, w_qkvg, cos_t, sin_t):
    B, S, _ = x.shape
    grid_row = lambda width: pl.BlockSpec((1, RET_PROJ_ROWS, width), lambda bi, i: (bi, i, 0))
    qk = jax.ShapeDtypeStruct((B, S, RET_QK_WIDTH), BF16)
    vg = jax.ShapeDtypeStruct((B, S, RET_V_WIDTH), BF16)
    return pl.pallas_call(
        _ret_proj_kernel,
        grid=(B, S // RET_PROJ_ROWS),
        in_specs=[grid_row(D_MODEL), _resident(w_qkvg.shape), grid_row(LANES), grid_row(LANES)],
        out_specs=[grid_row(RET_QK_WIDTH), grid_row(RET_QK_WIDTH),
                   grid_row(RET_V_WIDTH), grid_row(RET_V_WIDTH)],
        out_shape=[qk, qk, vg, vg],
        compiler_params=_params(2),
        name="ret_proj",
    )(x, w_qkvg, cos_t, sin_t)


def _ret_core_kernel(q_ref, k_ref, v_ref, sg_ref, o_ref, state_ref):
    C = RET_CHUNK

    @pl.when(pl.program_id(1) == 0)
    def _():
        state_ref[...] = jnp.zeros_like(state_ref)

    ri = lax.broadcasted_iota(jnp.int32, (C, C), 0)
    ci = lax.broadcasted_iota(jnp.int32, (C, C), 1)
    rel = (ri - ci).astype(F32)
    idx = lax.broadcasted_iota(jnp.int32, (C, 1), 0).astype(F32)
    for h in range(RET_HEADS):
        log_gamma = math.log(1.0 - 2.0 ** (-5.0 - h))
        d_intra = jnp.where(rel >= 0, jnp.exp(jnp.maximum(rel, 0.0) * log_gamma), 0.0)
        q_decay = jnp.exp((idx + 1.0) * log_gamma)
        k_decay = jnp.exp((C - 1.0 - idx) * log_gamma)
        chunk_decay = math.exp(C * log_gamma)
        qs = slice(h * RET_QK_DIM, (h + 1) * RET_QK_DIM)
        vs = slice(h * RET_V_DIM, (h + 1) * RET_V_DIM)
        for c in range(RET_STEP_ROWS // C):
            rows = slice(c * C, (c + 1) * C)
            qc = q_ref[0, rows, qs]
            kc = k_ref[0, rows, qs]
            vc = v_ref[0, rows, vs]
            state = state_ref[h]
            att = _dot_nt(qc, kc) * d_intra
            inner = _dot(att.astype(BF16), vc)
            cross = _dot(qc, state.astype(BF16)) * q_decay
            kd = (kc.astype(F32) * k_decay).astype(BF16)
            state_ref[h] = state * chunk_decay + _dot_tn(kd, vc)
            o = inner + cross
            mu = jnp.mean(o, axis=-1, keepdims=True)
            d = o - mu
            var = jnp.mean(d * d, axis=-1, keepdims=True)
            on = d * lax.rsqrt(var + LN_EPS)
            o_ref[0, rows, vs] = (sg_ref[0, rows, vs].astype(F32) * on).astype(BF16)


def _ret_core(q, k, v, sg):
    B, S, _ = q.shape
    blk = lambda width: pl.BlockSpec((1, RET_STEP_ROWS, width), lambda bi, i: (bi, i, 0))
    return pl.pallas_call(
        _ret_core_kernel,
        grid=(B, S // RET_STEP_ROWS),
        in_specs=[blk(RET_QK_WIDTH), blk(RET_QK_WIDTH), blk(RET_V_WIDTH), blk(RET_V_WIDTH)],
        out_specs=blk(RET_V_WIDTH),
        out_shape=jax.ShapeDtypeStruct((B, S, RET_V_WIDTH), BF16),
        scratch_shapes=[pltpu.VMEM((RET_HEADS, RET_QK_DIM, RET_V_DIM), F32)],
        compiler_params=_params(2),
        name="ret_core",
    )(q, k, v, sg)


def _da_rope_tables(pos):
    inv = 1.0 / (ROPE_THETA ** (jnp.arange(0, DA_HEAD_DIM, 2, dtype=F32) / DA_HEAD_DIM))
    ang = pos * inv[None, None, :]
    cos, sin = jnp.cos(ang), jnp.sin(ang)
    return (jnp.concatenate([cos, cos, cos, cos], axis=-1),
            jnp.concatenate([-sin, -sin, sin, sin], axis=-1))


def _ret_rope_tables(pos):
    inv = 1.0 / (ROPE_THETA ** jnp.linspace(0.0, 1.0, RET_QK_DIM // 2, dtype=F32))
    ang = pos * inv[None, None, :]
    return jnp.cos(ang), jnp.sin(ang)


def kernel(x, mem, positions, ffn_w_in, ffn_w_out, ln_g, ln_b, da_w_qkv, da_w_o, da_lam_q, da_lam_k, da_subln_g, pool_w, pool_b, pool_scale, ret_w_qkvg, ret_w_o, xa_wq, xa_wkv, xa_wo):
    B, S, D = x.shape
    T = B * S
    pos = positions.astype(F32)[..., None]
    da_cos, da_sin = _da_rope_tables(pos)
    ret_cos, ret_sin = _ret_rope_tables(pos)
    kv_all = _mem_kv(mem.reshape(-1, D), xa_wkv.astype(BF16)).reshape(DEPTH, B, mem.shape[1], 2 * D)
    ln_g = ln_g.reshape(DEPTH, 4, 1, D)
    ln_b = ln_b.reshape(DEPTH, 4, 1, D)

    for i in range(DEPTH):
        m, j = i % N_MIXERS, i // N_MIXERS
        x = _ffn_ln(x.reshape(T, D), ffn_w_in[i, 0].astype(BF16), ffn_w_out[i, 0].astype(BF16),
                    ln_g[i, 0], ln_b[i, 0]).reshape(B, S, D)
        if m == 0:
            lam_init = 0.8 - 0.6 * math.exp(-0.3 * i)
            w_qkv = da_w_qkv[j][:, _DA_COL_PERM].astype(BF16)
            q, k, v = _da_proj(x, w_qkv, da_cos, da_sin)
            o = _da_attn(q, k, v, da_lam_q[j], da_lam_k[j], da_subln_g[j].reshape(1, -1), lam_init)
            x = _proj_ln(o.reshape(T, -1), da_w_o[j].astype(BF16), x.reshape(T, D),
                         ln_g[i, 1], ln_b[i, 1]).reshape(B, S, D)
        elif m == 1:
            x = _pool_ln(x.reshape(T, D), S, pool_w[j].astype(BF16), pool_b[j].reshape(1, D),
                         pool_scale[j].reshape(1, D), ln_g[i, 1], ln_b[i, 1]).reshape(B, S, D)
        else:
            q, k, v, sg = _ret_proj(x, ret_w_qkvg[j].astype(BF16), ret_cos, ret_sin)
            o = _ret_core(q, k, v, sg)
            x = _proj_ln(o.reshape(T, -1), ret_w_o[j].astype(BF16), x.reshape(T, D),
                         ln_g[i, 1], ln_b[i, 1]).reshape(B, S, D)
        x = _xattn_ln(x, kv_all, i, xa_wq[i].astype(BF16), xa_wo[i].astype(BF16),
                      ln_g[i, 2], ln_b[i, 2])
        x = _ffn_ln(x.reshape(T, D), ffn_w_in[i, 1].astype(BF16), ffn_w_out[i, 1].astype(BF16),
                    ln_g[i, 3], ln_b[i, 3]).reshape(B, S, D)
    return x
```

```python
import functools
import math

import jax
import jax.numpy as jnp
from jax import lax
from jax.experimental import pallas as pl
from jax.experimental.pallas import tpu as pltpu

D_MODEL = 1024
DEPTH = 4
N_MIXERS = 3
DA_HEADS = 8
DA_HEAD_DIM = 64
DA_V_DIM = 2 * DA_HEAD_DIM
ROPE_THETA = 10000.0
POOL_WINDOWS = (2, 4, 8, 16)
POOL_GROUP_DIM = D_MODEL // len(POOL_WINDOWS)
POOL_HALO = 16
RET_HEADS = 4
RET_QK_DIM = D_MODEL // RET_HEADS
RET_V_DIM = 2 * RET_QK_DIM
RET_CHUNK = 128
XA_HEADS = 4
XA_HEAD_DIM = D_MODEL // XA_HEADS
D_FF = 2816
ALPHA = (2 * DEPTH) ** 0.25
LN_EPS = 1e-5
LOG2E = math.log2(math.e)

BF16 = jnp.bfloat16
F32 = jnp.float32

V7X_VMEM_LIMIT_BYTES = 56 * 1024 * 1024
LANES = 128


def _params(n_axes):
    return pltpu.CompilerParams(
        dimension_semantics=("arbitrary",) * n_axes,
        vmem_limit_bytes=V7X_VMEM_LIMIT_BYTES)


def _resident(tail, lead=()):
    shape = (None,) * len(lead) + tuple(tail)
    index = tuple(lead) + (0,) * len(tail)
    return pl.BlockSpec(shape, lambda *_: index, pipeline_mode=pl.Buffered(1))


def _layer_norm(y, g, b):
    mu = jnp.mean(y, axis=-1, keepdims=True)
    d = y - mu
    var = jnp.mean(d * d, axis=-1, keepdims=True)
    return d * lax.rsqrt(var + LN_EPS) * g + b


def _dot(a, b):
    return jnp.dot(a, b, preferred_element_type=F32)


def _dot_nt(a, b):
    return lax.dot_general(a, b, (((1,), (1,)), ((), ())), preferred_element_type=F32)


def _dot_tn(a, b):
    return lax.dot_general(a, b, (((0,), (0,)), ((), ())), preferred_element_type=F32)


def _matmul_kernel(a_ref, w_ref, o_ref):
    o_ref[...] = _dot(a_ref[...].astype(BF16), w_ref[...]).astype(o_ref.dtype)


def _mem_kv(mem2d, wkv):
    R = mem2d.shape[0]
    L, _, N = wkv.shape
    return pl.pallas_call(
        _matmul_kernel,
        grid=(L,),
        in_specs=[pl.BlockSpec((R, D_MODEL), lambda l: (0, 0)),
                  pl.BlockSpec((None, D_MODEL, N), lambda l: (l, 0, 0))],
        out_specs=pl.BlockSpec((None, R, N), lambda l: (l, 0, 0)),
        out_shape=jax.ShapeDtypeStruct((L, R, N), BF16),
        compiler_params=_params(1),
        name="mem_kv",
    )(mem2d, wkv)


FFN_ROWS = 512
FFN_CHUNK = 512


def _ffn_ln_kernel(x_ref, win_ref, wout_ref, g_ref, b_ref, o_ref):
    x = x_ref[...]
    xb = x.astype(BF16)
    acc = jnp.zeros(x.shape, F32)
    for c0 in range(0, D_FF, FFN_CHUNK):
        c1 = min(c0 + FFN_CHUNK, D_FF)
        hg = _dot(xb, win_ref[:, c0:c1])
        hu = _dot(xb, win_ref[:, D_FF + c0:D_FF + c1])
        act = (hg * jax.nn.sigmoid(hg) * hu).astype(BF16)
        acc = acc + _dot(act, wout_ref[c0:c1, :])
    o_ref[...] = _layer_norm(ALPHA * x + 0.5 * acc, g_ref[...], b_ref[...])


def _ffn_ln(x2d, w_in_all, w_out_all, ln_g, ln_b, layer, which, ln_idx):
    T = x2d.shape[0]
    row = pl.BlockSpec((FFN_ROWS, D_MODEL), lambda i: (i, 0))
    return pl.pallas_call(
        _ffn_ln_kernel,
        grid=(T // FFN_ROWS,),
        in_specs=[row,
                  _resident((D_MODEL, 2 * D_FF), (layer, which)),
                  _resident((D_FF, D_MODEL), (layer, which)),
                  _resident((1, D_MODEL), (layer, ln_idx)),
                  _resident((1, D_MODEL), (layer, ln_idx))],
        out_specs=row,
        out_shape=jax.ShapeDtypeStruct((T, D_MODEL), F32),
        compiler_params=_params(1),
        name="ffn_ln",
    )(x2d, w_in_all, w_out_all, ln_g, ln_b)


PROJ_ROWS = 512


def _proj_ln_kernel(a_ref, w_ref, x_ref, g_ref, b_ref, o_ref):
    h = _dot(a_ref[...], w_ref[...])
    o_ref[...] = _layer_norm(ALPHA * x_ref[...] + h, g_ref[...], b_ref[...])


def _proj_ln(a2d, w_all, j, x2d, ln_g, ln_b, layer):
    T, K = a2d.shape
    return pl.pallas_call(
        _proj_ln_kernel,
        grid=(T // PROJ_ROWS,),
        in_specs=[pl.BlockSpec((PROJ_ROWS, K), lambda i: (i, 0)),
                  _resident((K, D_MODEL), (j,)),
                  pl.BlockSpec((PROJ_ROWS, D_MODEL), lambda i: (i, 0)),
                  _resident((1, D_MODEL), (layer, 1)), _resident((1, D_MODEL), (layer, 1))],
        out_specs=pl.BlockSpec((PROJ_ROWS, D_MODEL), lambda i: (i, 0)),
        out_shape=jax.ShapeDtypeStruct((T, D_MODEL), F32),
        compiler_params=_params(1),
        name="proj_ln",
    )(a2d, w_all, x2d, ln_g, ln_b)


XA_ROWS = 1024
XA_SUB = 256


def _xattn_ln_kernel(x_ref, kv_ref, wq_ref, wo_ref, g_ref, b_ref, o_ref):
    for r0 in range(0, XA_ROWS, XA_SUB):
        x = x_ref[0, r0:r0 + XA_SUB, :]
        q = (_dot(x.astype(BF16), wq_ref[...]) * (XA_HEAD_DIM ** -0.5 * LOG2E)).astype(BF16)
        heads = []
        for h in range(XA_HEADS):
            lo, hi = h * XA_HEAD_DIM, (h + 1) * XA_HEAD_DIM
            s = _dot_nt(q[:, lo:hi], kv_ref[:, lo:hi])
            e = jnp.exp2(s - jnp.max(s, axis=-1, keepdims=True))
            p = e * (1.0 / jnp.sum(e, axis=-1, keepdims=True))
            heads.append(_dot(p.astype(BF16), kv_ref[:, D_MODEL + lo:D_MODEL + hi]).astype(BF16))
        h_out = _dot(jnp.concatenate(heads, axis=-1), wo_ref[...])
        o_ref[0, r0:r0 + XA_SUB, :] = _layer_norm(ALPHA * x + h_out, g_ref[...], b_ref[...])


def _xattn_ln(x, kv_all, wq_all, wo_all, ln_g, ln_b, layer):
    B, S, _ = x.shape
    M = kv_all.shape[2]
    row = pl.BlockSpec((1, XA_ROWS, D_MODEL), lambda bi, i: (bi, i, 0))
    return pl.pallas_call(
        _xattn_ln_kernel,
        grid=(B, S // XA_ROWS),
        in_specs=[row,
                  pl.BlockSpec((None, None, M, 2 * D_MODEL), lambda bi, i: (layer, bi, 0, 0)),
                  _resident((D_MODEL, D_MODEL), (layer,)), _resident((D_MODEL, D_MODEL), (layer,)),
                  _resident((1, D_MODEL), (layer, 2)), _resident((1, D_MODEL), (layer, 2))],
        out_specs=row,
        out_shape=jax.ShapeDtypeStruct((B, S, D_MODEL), F32),
        compiler_params=_params(2),
        name="xattn_ln",
    )(x, kv_all, wq_all, wo_all, ln_g, ln_b)


DA_PROJ_ROWS = 512
DA_QBLOCK = 256
DA_WIDTH = DA_HEADS * DA_V_DIM


def _da_proj_kernel(x_ref, w_ref, cos_ref, sin_ref, q_ref, k_ref, v_ref):
    xb = x_ref[0].astype(BF16)
    cos = cos_ref[0]
    sin = sin_ref[0]
    lane = lax.broadcasted_iota(jnp.int32, (1, DA_V_DIM), 1)
    first_half = (lane % DA_HEAD_DIM) < (DA_HEAD_DIM // 2)
    q_scale = DA_HEAD_DIM ** -0.5 * LOG2E
    for part, out_ref, scale in ((0, q_ref, q_scale), (1, k_ref, 1.0)):
        y = _dot(xb, w_ref[:, part * DA_WIDTH:(part + 1) * DA_WIDTH])
        for h in range(DA_HEADS):
            yh = y[:, h * DA_V_DIM:(h + 1) * DA_V_DIM]
            partner = jnp.where(first_half,
                                pltpu.roll(yh, DA_V_DIM - DA_HEAD_DIM // 2, axis=1),
                                pltpu.roll(yh, DA_HEAD_DIM // 2, axis=1))
            r = yh * cos + partner * sin
            out_ref[0, :, h * DA_V_DIM:(h + 1) * DA_V_DIM] = (r * scale).astype(BF16)
    v_ref[0] = _dot(xb, w_ref[:, 2 * DA_WIDTH:]).astype(BF16)


def _da_proj(x, w_qkv_all, j, cos_t, sin_t):
    B, S, _ = x.shape
    row = pl.BlockSpec((1, DA_PROJ_ROWS, D_MODEL), lambda bi, i: (bi, i, 0))
    tab = pl.BlockSpec((1, DA_PROJ_ROWS, LANES), lambda bi, i: (bi, i, 0))
    out = jax.ShapeDtypeStruct((B, S, DA_WIDTH), BF16)
    return pl.pallas_call(
        _da_proj_kernel,
        grid=(B, S // DA_PROJ_ROWS),
        in_specs=[row, _resident((D_MODEL, 3 * DA_WIDTH), (j,)), tab, tab],
        out_specs=[row, row, row],
        out_shape=[out, out, out],
        compiler_params=_params(2),
        name="da_proj",
    )(x, w_qkv_all, cos_t, sin_t)


DA_ONES_ROWS = 16


def _da_attn_kernel(q_ref, k_ref, v_ref, lq_ref, lk_ref, sg_ref, o_ref, vt_ref, *, lam_init):
    S = q_ref.shape[1]
    tq = DA_QBLOCK
    vt_ref[:DA_V_DIM, :] = v_ref[0].astype(F32).T.astype(BF16)
    vt_ref[DA_V_DIM:, :] = jnp.ones((DA_ONES_ROWS, S), BF16)
    lqk = lq_ref[...] * lk_ref[...]
    lam = (jnp.exp(jnp.sum(lqk[0:1], axis=-1, keepdims=True))
           - jnp.exp(jnp.sum(lqk[1:2], axis=-1, keepdims=True)) + lam_init)
    lane = lax.broadcasted_iota(jnp.int32, (1, DA_V_DIM), 1)
    map0 = lane < DA_HEAD_DIM
    sub_g = sg_ref[...] * (1.0 - lam_init)
    key = lax.broadcasted_iota(jnp.int32, (tq, 2 * tq), 0)
    qry = lax.broadcasted_iota(jnp.int32, (tq, 2 * tq), 1)
    causal = key <= jnp.where(qry >= tq, qry - tq, qry)
    neg = jnp.finfo(F32).min

    def scores(i):
        n0 = i * tq
        q = q_ref[0, n0:n0 + tq, :]
        zero = jnp.zeros_like(q)
        q2 = jnp.concatenate([jnp.where(map0, q, zero), jnp.where(map0, zero, q)], axis=0)
        s_d = jnp.where(causal, _dot_nt(k_ref[0, n0:n0 + tq, :], q2), neg)
        s_p = _dot_nt(k_ref[0, :n0, :], q2) if i > 0 else None
        return s_p, s_d

    def finish(i, s_p, s_d):
        n0 = i * tq
        m = jnp.max(s_d, axis=0, keepdims=True)
        if i > 0:
            m = jnp.maximum(m, jnp.max(s_p, axis=0, keepdims=True))
            acc = _dot(vt_ref[:, :n0], jnp.exp2(s_p - m).astype(BF16))
            acc = acc + _dot(vt_ref[:, n0:n0 + tq], jnp.exp2(s_d - m).astype(BF16))
        else:
            acc = _dot(vt_ref[:, n0:n0 + tq], jnp.exp2(s_d - m).astype(BF16))
        r = acc[:DA_V_DIM] / acc[DA_V_DIM:DA_V_DIM + 1]
        o = r[:, :tq] - lam * r[:, tq:]
        o = o * lax.rsqrt(jnp.mean(o * o, axis=0, keepdims=True) + LN_EPS) * sub_g
        o_ref[0, n0:n0 + tq, :] = o.T.astype(BF16)

    nb = S // tq
    pending = scores(0)
    for i in range(nb):
        upcoming = scores(i + 1) if i + 1 < nb else None
        finish(i, *pending)
        pending = upcoming


def _da_attn(q, k, v, lam_q, lam_k, subln_g, j, lam_init):
    B, S, _ = q.shape
    head = pl.BlockSpec((1, S, DA_V_DIM), lambda bi, h: (bi, 0, h))
    return pl.pallas_call(
        functools.partial(_da_attn_kernel, lam_init=lam_init),
        grid=(B, DA_HEADS),
        in_specs=[head, head, head, _resident((2, DA_HEAD_DIM), (j,)),
                  _resident((2, DA_HEAD_DIM), (j,)), _resident((DA_V_DIM, 1), (j,))],
        out_specs=head,
        out_shape=jax.ShapeDtypeStruct((B, S, DA_WIDTH), BF16),
        scratch_shapes=[pltpu.VMEM((DA_V_DIM + DA_ONES_ROWS, S), BF16)],
        compiler_params=_params(2),
        name="da_attn",
    )(q, k, v, lam_q, lam_k, subln_g)


POOL_ROWS = 512


def _pool_ln_kernel(x_ref, prev_ref, w_ref, bias_ref, scale_ref, g_ref, b_ref, o_ref, buf_ref,
                    *, tiles_per_seq):
    i = pl.program_id(0)
    t0 = (i % tiles_per_seq) * POOL_ROWS
    x = x_ref[...]
    first = (i % tiles_per_seq) == 0
    buf_ref[0:POOL_HALO, :] = jnp.where(first, 0.0, prev_ref[...])
    buf_ref[POOL_HALO:, :] = x
    t = t0 + lax.broadcasted_iota(jnp.int32, (POOL_ROWS, 1), 0)
    outs = []
    for gi, w in enumerate(POOL_WINDOWS):
        lo, hi = gi * POOL_GROUP_DIM, (gi + 1) * POOL_GROUP_DIM
        xg = x[:, lo:hi]
        tot = xg
        for k in range(1, w):
            tot = tot + buf_ref[POOL_HALO - k:POOL_HALO - k + POOL_ROWS, lo:hi]
        cnt = jnp.minimum(t + 1, w).astype(F32)
        pooled = tot / cnt - xg
        outs.append(_dot(pooled.astype(BF16), w_ref[gi]))
    y = (jnp.concatenate(outs, axis=-1) + bias_ref[...]) * scale_ref[...]
    o_ref[...] = _layer_norm(ALPHA * x + y, g_ref[...], b_ref[...])


def _pool_ln(x2d, seq_len, w_all, bias_all, scale_all, j, ln_g, ln_b, layer):
    T = x2d.shape[0]
    tiles_per_seq = seq_len // POOL_ROWS
    halo_blocks = POOL_ROWS // POOL_HALO
    return pl.pallas_call(
        functools.partial(_pool_ln_kernel, tiles_per_seq=tiles_per_seq),
        grid=(T // POOL_ROWS,),
        in_specs=[pl.BlockSpec((POOL_ROWS, D_MODEL), lambda i: (i, 0)),
                  pl.BlockSpec((POOL_HALO, D_MODEL),
                               lambda i: (jnp.maximum(i * halo_blocks - 1, 0), 0)),
                  _resident(w_all.shape[1:], (j,)),
                  _resident((1, D_MODEL), (j,)), _resident((1, D_MODEL), (j,)),
                  _resident((1, D_MODEL), (layer, 1)), _resident((1, D_MODEL), (layer, 1))],
        out_specs=pl.BlockSpec((POOL_ROWS, D_MODEL), lambda i: (i, 0)),
        out_shape=jax.ShapeDtypeStruct((T, D_MODEL), F32),
        scratch_shapes=[pltpu.VMEM((POOL_HALO + POOL_ROWS, D_MODEL), F32)],
        compiler_params=_params(1),
        name="pool_ln",
    )(x2d, x2d, w_all, bias_all, scale_all, ln_g, ln_b)


RET_PROJ_ROWS = 512
RET_QK_WIDTH = RET_HEADS * RET_QK_DIM
RET_V_WIDTH = RET_HEADS * RET_V_DIM
RET_STEP_ROWS = 512


def _ret_proj_kernel(x_ref, w_ref, cos_ref, sin_ref, q_ref, k_ref, v_ref, sg_ref):
    xb = x_ref[0].astype(BF16)
    cos = cos_ref[0]
    sin = sin_ref[0]
    half = RET_QK_DIM // 2
    for part, out_ref, scale in ((0, q_ref, 1.0), (1, k_ref, RET_QK_DIM ** -0.5)):
        y = _dot(xb, w_ref[:, part * RET_QK_WIDTH:(part + 1) * RET_QK_WIDTH])
        for h in range(RET_HEADS):
            x1 = y[:, h * RET_QK_DIM:h * RET_QK_DIM + half]
            x2 = y[:, h * RET_QK_DIM + half:(h + 1) * RET_QK_DIM]
            out_ref[0, :, h * RET_QK_DIM:h * RET_QK_DIM + half] = (
                (x1 * cos - x2 * sin) * scale).astype(BF16)
            out_ref[0, :, h * RET_QK_DIM + half:(h + 1) * RET_QK_DIM] = (
                (x2 * cos + x1 * sin) * scale).astype(BF16)
    v0 = 2 * RET_QK_WIDTH
    g0 = v0 + RET_V_WIDTH
    for c in range(0, RET_V_WIDTH, 1024):
        v_ref[0, :, c:c + 1024] = _dot(xb, w_ref[:, v0 + c:v0 + c + 1024]).astype(BF16)
        gate = _dot(xb, w_ref[:, g0 + c:g0 + c + 1024])
        sg_ref[0, :, c:c + 1024] = (gate * jax.nn.sigmoid(gate)).astype(BF16)


def _ret_proj(x, w_all, j, cos_t, sin_t):
    B, S, _ = x.shape
    grid_row = lambda width: pl.BlockSpec((1, RET_PROJ_ROWS, width), lambda bi, i: (bi, i, 0))
    qk = jax.ShapeDtypeStruct((B, S, RET_QK_WIDTH), BF16)
    vg = jax.ShapeDtypeStruct((B, S, RET_V_WIDTH), BF16)
    return pl.pallas_call(
        _ret_proj_kernel,
        grid=(B, S // RET_PROJ_ROWS),
        in_specs=[grid_row(D_MODEL), _resident(w_all.shape[1:], (j,)),
                  grid_row(LANES), grid_row(LANES)],
        out_specs=[grid_row(RET_QK_WIDTH), grid_row(RET_QK_WIDTH),
                   grid_row(RET_V_WIDTH), grid_row(RET_V_WIDTH)],
        out_shape=[qk, qk, vg, vg],
        compiler_params=_params(2),
        name="ret_proj",
    )(x, w_all, cos_t, sin_t)


def _ret_core_kernel(q_ref, k_ref, v_ref, sg_ref, o_ref, state_ref):
    C = RET_CHUNK

    @pl.when(pl.program_id(1) == 0)
    def _():
        state_ref[...] = jnp.zeros_like(state_ref)

    ri = lax.broadcasted_iota(jnp.int32, (C, C), 0)
    ci = lax.broadcasted_iota(jnp.int32, (C, C), 1)
    rel = (ri - ci).astype(F32)
    idx = lax.broadcasted_iota(jnp.int32, (C, 1), 0).astype(F32)
    decays = []
    for h in range(RET_HEADS):
        log_gamma = math.log(1.0 - 2.0 ** (-5.0 - h))
        decays.append((jnp.where(rel >= 0, jnp.exp(jnp.maximum(rel, 0.0) * log_gamma), 0.0),
                       jnp.exp((idx + 1.0) * log_gamma),
                       jnp.exp((C - 1.0 - idx) * log_gamma),
                       math.exp(C * log_gamma)))
    for c in range(RET_STEP_ROWS // C):
        rows = slice(c * C, (c + 1) * C)
        for h in range(RET_HEADS):
            d_intra, q_decay, k_decay, chunk_decay = decays[h]
            qs = slice(h * RET_QK_DIM, (h + 1) * RET_QK_DIM)
            vs = slice(h * RET_V_DIM, (h + 1) * RET_V_DIM)
            qc = q_ref[0, rows, qs]
            kc = k_ref[0, rows, qs]
            vc = v_ref[0, rows, vs]
            state = state_ref[h]
            att = _dot_nt(qc, kc) * d_intra
            inner = _dot(att.astype(BF16), vc)
            cross = _dot(qc, state.astype(BF16)) * q_decay
            kd = (kc.astype(F32) * k_decay).astype(BF16)
            state_ref[h] = state * chunk_decay + _dot_tn(kd, vc)
            o = inner + cross
            mu = jnp.mean(o, axis=-1, keepdims=True)
            d = o - mu
            var = jnp.mean(d * d, axis=-1, keepdims=True)
            on = d * lax.rsqrt(var + LN_EPS)
            o_ref[0, rows, vs] = (sg_ref[0, rows, vs].astype(F32) * on).astype(BF16)


def _ret_core(q, k, v, sg):
    B, S, _ = q.shape
    blk = lambda width: pl.BlockSpec((1, RET_STEP_ROWS, width), lambda bi, i: (bi, i, 0))
    return pl.pallas_call(
        _ret_core_kernel,
        grid=(B, S // RET_STEP_ROWS),
        in_specs=[blk(RET_QK_WIDTH), blk(RET_QK_WIDTH), blk(RET_V_WIDTH), blk(RET_V_WIDTH)],
        out_specs=blk(RET_V_WIDTH),
        out_shape=jax.ShapeDtypeStruct((B, S, RET_V_WIDTH), BF16),
        scratch_shapes=[pltpu.VMEM((RET_HEADS, RET_QK_DIM, RET_V_DIM), F32)],
        compiler_params=_params(2),
        name="ret_core",
    )(q, k, v, sg)


def _da_rope_tables(pos):
    inv = 1.0 / (ROPE_THETA ** (jnp.arange(0, DA_HEAD_DIM, 2, dtype=F32) / DA_HEAD_DIM))
    ang = pos * inv[None, None, :]
    cos, sin = jnp.cos(ang), jnp.sin(ang)
    return (jnp.concatenate([cos, cos, cos, cos], axis=-1),
            jnp.concatenate([-sin, sin, -sin, sin], axis=-1))


def _ret_rope_tables(pos):
    inv = 1.0 / (ROPE_THETA ** jnp.linspace(0.0, 1.0, RET_QK_DIM // 2, dtype=F32))
    ang = pos * inv[None, None, :]
    return jnp.cos(ang), jnp.sin(ang)


def kernel(x, mem, positions, ffn_w_in, ffn_w_out, ln_g, ln_b, da_w_qkv, da_w_o, da_lam_q, da_lam_k, da_subln_g, pool_w, pool_b, pool_scale, ret_w_qkvg, ret_w_o, xa_wq, xa_wkv, xa_wo):
    B, S, D = x.shape
    T = B * S
    pos = positions.astype(F32)[..., None]
    da_cos, da_sin = _da_rope_tables(pos)
    ret_cos, ret_sin = _ret_rope_tables(pos)
    kv_all = _mem_kv(mem.reshape(-1, D), xa_wkv.astype(BF16)).reshape(DEPTH, B, mem.shape[1], 2 * D)
    ln_g = ln_g.reshape(DEPTH, 4, 1, D)
    ln_b = ln_b.reshape(DEPTH, 4, 1, D)
    ffn_w_in, ffn_w_out = ffn_w_in.astype(BF16), ffn_w_out.astype(BF16)
    xa_wq, xa_wo = xa_wq.astype(BF16), xa_wo.astype(BF16)
    da_w_qkv, da_w_o = da_w_qkv.astype(BF16), da_w_o.astype(BF16)
    ret_w_qkvg, ret_w_o = ret_w_qkvg.astype(BF16), ret_w_o.astype(BF16)
    pool_w = pool_w.astype(BF16)
    pool_b = pool_b.reshape(-1, 1, D)
    pool_scale = pool_scale.reshape(-1, 1, D)
    da_subln_g = da_subln_g.reshape(-1, DA_V_DIM, 1)

    x = x.reshape(T, D)
    for i in range(DEPTH):
        m, j = i % N_MIXERS, i // N_MIXERS
        x = _ffn_ln(x, ffn_w_in, ffn_w_out, ln_g, ln_b, i, 0, 0)
        if m == 0:
            lam_init = 0.8 - 0.6 * math.exp(-0.3 * i)
            q, k, v = _da_proj(x.reshape(B, S, D), da_w_qkv, j, da_cos, da_sin)
            o = _da_attn(q, k, v, da_lam_q, da_lam_k, da_subln_g, j, lam_init)
            x = _proj_ln(o.reshape(T, -1), da_w_o, j, x, ln_g, ln_b, i)
        elif m == 1:
            x = _pool_ln(x, S, pool_w, pool_b, pool_scale, j, ln_g, ln_b, i)
        else:
            q, k, v, sg = _ret_proj(x.reshape(B, S, D), ret_w_qkvg, j, ret_cos, ret_sin)
            o = _ret_core(q, k, v, sg)
            x = _proj_ln(o.reshape(T, -1), ret_w_o, j, x, ln_g, ln_b, i)
        x = _xattn_ln(x.reshape(B, S, D), kv_all, xa_wq, xa_wo, ln_g, ln_b, i).reshape(T, D)
        x = _ffn_ln(x, ffn_w_in, ffn_w_out, ln_g, ln_b, i, 1, 3)
    return x.reshape(B, S, D)
```

```python
import functools
import math

import jax
import jax.numpy as jnp
from jax import lax
from jax.experimental import pallas as pl
from jax.experimental.pallas import tpu as pltpu

D_MODEL = 1024
DEPTH = 4
N_MIXERS = 3
DA_HEADS = 8
DA_HEAD_DIM = 64
DA_V_DIM = 2 * DA_HEAD_DIM
ROPE_THETA = 10000.0
POOL_WINDOWS = (2, 4, 8, 16)
POOL_GROUP_DIM = D_MODEL // len(POOL_WINDOWS)
POOL_HALO = 32
RET_HEADS = 4
RET_QK_DIM = D_MODEL // RET_HEADS
RET_V_DIM = 2 * RET_QK_DIM
RET_CHUNK = 128
XA_HEADS = 4
XA_HEAD_DIM = D_MODEL // XA_HEADS
D_FF = 2816
ALPHA = (2 * DEPTH) ** 0.25
LN_EPS = 1e-5
LOG2E = math.log2(math.e)

BF16 = jnp.bfloat16
F32 = jnp.float32

V7X_VMEM_LIMIT_BYTES = 56 * 1024 * 1024
LANES = 128


def _params(n_axes):
    return pltpu.CompilerParams(
        dimension_semantics=("arbitrary",) * n_axes,
        vmem_limit_bytes=V7X_VMEM_LIMIT_BYTES)


def _resident(tail, lead=()):
    shape = (None,) * len(lead) + tuple(tail)
    index = tuple(lead) + (0,) * len(tail)
    return pl.BlockSpec(shape, lambda *_: index, pipeline_mode=pl.Buffered(1))


def _layer_norm(y, g, b):
    mu = jnp.mean(y, axis=-1, keepdims=True)
    d = y - mu
    var = jnp.mean(d * d, axis=-1, keepdims=True)
    return d * lax.rsqrt(var + LN_EPS) * g + b


def _dot(a, b):
    return jnp.dot(a, b, preferred_element_type=F32)


def _dot_nt(a, b):
    return lax.dot_general(a, b, (((1,), (1,)), ((), ())), preferred_element_type=F32)


def _dot_tn(a, b):
    return lax.dot_general(a, b, (((0,), (0,)), ((), ())), preferred_element_type=F32)


def _matmul_kernel(a_ref, w_ref, o_ref):
    o_ref[...] = _dot(a_ref[...].astype(BF16), w_ref[...].astype(BF16)).astype(o_ref.dtype)


def _mem_kv(mem2d, wkv):
    R = mem2d.shape[0]
    L, _, N = wkv.shape
    return pl.pallas_call(
        _matmul_kernel,
        grid=(L,),
        in_specs=[_resident((R, D_MODEL)),
                  pl.BlockSpec((None, D_MODEL, N), lambda l: (l, 0, 0))],
        out_specs=pl.BlockSpec((None, R, N), lambda l: (l, 0, 0)),
        out_shape=jax.ShapeDtypeStruct((L, R, N), BF16),
        compiler_params=_params(1),
        name="mem_kv",
    )(mem2d, wkv)


FFN_ROWS = 512
FFN_CHUNK = 512


BF16_SUBLANES = 16


def _cast_rows_form(rows, cols, steps):
    granule = steps * BF16_SUBLANES
    if rows % granule == 0:
        return rows, cols
    assert (rows * cols) % (granule * LANES) == 0
    return granule, rows * cols // granule


def _ffn_ln_kernel(*refs, n_cast):
    x_ref, win_ref, wout_ref, g_ref, b_ref = refs[:5]
    o_ref = refs[5 + n_cast]
    for src_ref, dst_ref in zip(refs[5:5 + n_cast], refs[6 + n_cast:]):
        dst_ref[...] = src_ref[...].astype(BF16)
    x = x_ref[...]
    xb = x.astype(BF16)
    acc = jnp.zeros(x.shape, F32)
    for c0 in range(0, D_FF, FFN_CHUNK):
        c1 = min(c0 + FFN_CHUNK, D_FF)
        hg = _dot(xb, win_ref[:, c0:c1])
        hu = _dot(xb, win_ref[:, D_FF + c0:D_FF + c1])
        act = (hg * jax.nn.sigmoid(hg) * hu).astype(BF16)
        acc = acc + _dot(act, wout_ref[c0:c1, :])
    o_ref[...] = _layer_norm(ALPHA * x + 0.5 * acc, g_ref[...], b_ref[...])


def _ffn_ln(x2d, w_in, w_out, ln_g, ln_b, layer, ln_idx, to_cast):
    T = x2d.shape[0]
    steps = T // FFN_ROWS
    row = pl.BlockSpec((FFN_ROWS, D_MODEL), lambda i: (i, 0))
    cast_in, cast_specs_in, cast_specs_out, cast_shapes, final_shapes = [], [], [], [], []
    for arr, lead in to_cast:
        rows, cols = arr.shape[-2:]
        r, c = _cast_rows_form(rows, cols, steps)
        lead = tuple(lead)
        cast_in.append(arr.reshape(arr.shape[:-2] + (r, c)))
        cast_specs_in.append(pl.BlockSpec((None,) * len(lead) + (r // steps, c),
                                          lambda i, lead=lead: lead + (i, 0)))
        cast_specs_out.append(pl.BlockSpec((r // steps, c), lambda i: (i, 0)))
        cast_shapes.append(jax.ShapeDtypeStruct((r, c), BF16))
        final_shapes.append((rows, cols))
    outs = pl.pallas_call(
        functools.partial(_ffn_ln_kernel, n_cast=len(to_cast)),
        grid=(steps,),
        in_specs=[row,
                  _resident((D_MODEL, 2 * D_FF)),
                  _resident((D_FF, D_MODEL)),
                  _resident((1, D_MODEL), (layer, ln_idx)),
                  _resident((1, D_MODEL), (layer, ln_idx))] + cast_specs_in,
        out_specs=[row] + cast_specs_out,
        out_shape=[jax.ShapeDtypeStruct((T, D_MODEL), F32)] + cast_shapes,
        compiler_params=_params(1),
        name="ffn_ln",
    )(x2d, w_in, w_out, ln_g, ln_b, *cast_in)
    return outs[0], [o.reshape(s) for o, s in zip(outs[1:], final_shapes)]


PROJ_ROWS = 512


def _proj_ln_kernel(a_ref, w_ref, x_ref, g_ref, b_ref, o_ref):
    h = _dot(a_ref[...], w_ref[...])
    o_ref[...] = _layer_norm(ALPHA * x_ref[...] + h, g_ref[...], b_ref[...])


def _proj_ln(a2d, w, x2d, ln_g, ln_b, layer):
    T, K = a2d.shape
    return pl.pallas_call(
        _proj_ln_kernel,
        grid=(T // PROJ_ROWS,),
        in_specs=[pl.BlockSpec((PROJ_ROWS, K), lambda i: (i, 0)),
                  _resident((K, D_MODEL)),
                  pl.BlockSpec((PROJ_ROWS, D_MODEL), lambda i: (i, 0)),
                  _resident((1, D_MODEL), (layer, 1)), _resident((1, D_MODEL), (layer, 1))],
        out_specs=pl.BlockSpec((PROJ_ROWS, D_MODEL), lambda i: (i, 0)),
        out_shape=jax.ShapeDtypeStruct((T, D_MODEL), F32),
        compiler_params=_params(1),
        name="proj_ln",
    )(a2d, w, x2d, ln_g, ln_b)


XA_ROWS = 1024
XA_SUB = 256


def _xattn_ln_kernel(x_ref, kv_ref, wq_ref, wo_ref, g_ref, b_ref, o_ref):
    def project(r0):
        x = x_ref[0, r0:r0 + XA_SUB, :]
        return x, (_dot(x.astype(BF16), wq_ref[...]) * (XA_HEAD_DIM ** -0.5 * LOG2E)).astype(BF16)

    def finish(r0, x, h_out):
        o_ref[0, r0:r0 + XA_SUB, :] = _layer_norm(ALPHA * x + h_out, g_ref[...], b_ref[...])

    starts = list(range(0, XA_ROWS, XA_SUB))
    x, q = project(starts[0])
    lagged = None
    for n, r0 in enumerate(starts):
        s = [_dot_nt(q[:, h * XA_HEAD_DIM:(h + 1) * XA_HEAD_DIM],
                     kv_ref[:, h * XA_HEAD_DIM:(h + 1) * XA_HEAD_DIM]) for h in range(XA_HEADS)]
        upcoming = project(starts[n + 1]) if n + 1 < len(starts) else None
        heads = []
        for h in range(XA_HEADS):
            e = jnp.exp2(s[h] - jnp.max(s[h], axis=-1, keepdims=True))
            p = e * (1.0 / jnp.sum(e, axis=-1, keepdims=True))
            lo = D_MODEL + h * XA_HEAD_DIM
            heads.append(_dot(p.astype(BF16), kv_ref[:, lo:lo + XA_HEAD_DIM]).astype(BF16))
        if lagged is not None:
            finish(*lagged)
        lagged = (r0, x, _dot(jnp.concatenate(heads, axis=-1), wo_ref[...]))
        if upcoming is not None:
            x, q = upcoming
    finish(*lagged)


def _xattn_ln(x, kv_all, wq, wo, ln_g, ln_b, layer):
    B, S, _ = x.shape
    M = kv_all.shape[2]
    row = pl.BlockSpec((1, XA_ROWS, D_MODEL), lambda bi, i: (bi, i, 0))
    return pl.pallas_call(
        _xattn_ln_kernel,
        grid=(B, S // XA_ROWS),
        in_specs=[row,
                  pl.BlockSpec((None, None, M, 2 * D_MODEL), lambda bi, i: (layer, bi, 0, 0)),
                  _resident((D_MODEL, D_MODEL)), _resident((D_MODEL, D_MODEL)),
                  _resident((1, D_MODEL), (layer, 2)), _resident((1, D_MODEL), (layer, 2))],
        out_specs=row,
        out_shape=jax.ShapeDtypeStruct((B, S, D_MODEL), F32),
        compiler_params=_params(2),
        name="xattn_ln",
    )(x, kv_all, wq, wo, ln_g, ln_b)


DA_PROJ_ROWS = 512
DA_QBLOCK = 256
DA_WIDTH = DA_HEADS * DA_V_DIM


def _da_proj_kernel(x_ref, w_ref, cos_ref, sin_ref, q_ref, k_ref, v_ref):
    xb = x_ref[0].astype(BF16)
    cos = cos_ref[0]
    sin = sin_ref[0]
    lane = lax.broadcasted_iota(jnp.int32, (1, DA_V_DIM), 1)
    first_half = (lane % DA_HEAD_DIM) < (DA_HEAD_DIM // 2)
    q_scale = DA_HEAD_DIM ** -0.5 * LOG2E
    for part, out_ref, scale in ((0, q_ref, q_scale), (1, k_ref, 1.0)):
        y = _dot(xb, w_ref[:, part * DA_WIDTH:(part + 1) * DA_WIDTH])
        for h in range(DA_HEADS):
            yh = y[:, h * DA_V_DIM:(h + 1) * DA_V_DIM]
            partner = jnp.where(first_half,
                                pltpu.roll(yh, DA_V_DIM - DA_HEAD_DIM // 2, axis=1),
                                pltpu.roll(yh, DA_HEAD_DIM // 2, axis=1))
            r = yh * cos + partner * sin
            out_ref[0, :, h * DA_V_DIM:(h + 1) * DA_V_DIM] = (r * scale).astype(BF16)
    v_ref[0] = _dot(xb, w_ref[:, 2 * DA_WIDTH:]).astype(BF16)


def _da_proj(x, w_qkv, cos_t, sin_t):
    B, S, _ = x.shape
    row = pl.BlockSpec((1, DA_PROJ_ROWS, D_MODEL), lambda bi, i: (bi, i, 0))
    tab = pl.BlockSpec((1, DA_PROJ_ROWS, LANES), lambda bi, i: (bi, i, 0))
    out = jax.ShapeDtypeStruct((B, S, DA_WIDTH), BF16)
    return pl.pallas_call(
        _da_proj_kernel,
        grid=(B, S // DA_PROJ_ROWS),
        in_specs=[row, _resident((D_MODEL, 3 * DA_WIDTH)), tab, tab],
        out_specs=[row, row, row],
        out_shape=[out, out, out],
        compiler_params=_params(2),
        name="da_proj",
    )(x, w_qkv, cos_t, sin_t)


DA_ONES_ROWS = 16


def _da_attn_kernel(q_ref, k_ref, v_ref, lq_ref, lk_ref, sg_ref, o_ref, vt_ref, *, lam_init):
    S = q_ref.shape[1]
    tq = DA_QBLOCK
    vt_ref[:DA_V_DIM, :] = v_ref[0].astype(F32).T.astype(BF16)
    vt_ref[DA_V_DIM:, :] = jnp.ones((DA_ONES_ROWS, S), BF16)
    lqk = lq_ref[...] * lk_ref[...]
    lam = (jnp.exp(jnp.sum(lqk[0:1], axis=-1, keepdims=True))
           - jnp.exp(jnp.sum(lqk[1:2], axis=-1, keepdims=True)) + lam_init)
    lane = lax.broadcasted_iota(jnp.int32, (1, DA_V_DIM), 1)
    map0 = lane < DA_HEAD_DIM
    sub_g = sg_ref[...] * (1.0 - lam_init)
    key = lax.broadcasted_iota(jnp.int32, (tq, 2 * tq), 0)
    qry = lax.broadcasted_iota(jnp.int32, (tq, 2 * tq), 1)
    causal = key <= jnp.where(qry >= tq, qry - tq, qry)
    neg = jnp.finfo(F32).min

    def scores(i):
        n0 = i * tq
        q = q_ref[0, n0:n0 + tq, :]
        zero = jnp.zeros_like(q)
        q2 = jnp.concatenate([jnp.where(map0, q, zero), jnp.where(map0, zero, q)], axis=0)
        s_d = jnp.where(causal, _dot_nt(k_ref[0, n0:n0 + tq, :], q2), neg)
        s_p = _dot_nt(k_ref[0, :n0, :], q2) if i > 0 else None
        return s_p, s_d

    def finish(i, s_p, s_d):
        n0 = i * tq
        m = jnp.max(s_d, axis=0, keepdims=True)
        if i > 0:
            m = jnp.maximum(m, jnp.max(s_p, axis=0, keepdims=True))
            acc = _dot(vt_ref[:, :n0], jnp.exp2(s_p - m).astype(BF16))
            acc = acc + _dot(vt_ref[:, n0:n0 + tq], jnp.exp2(s_d - m).astype(BF16))
        else:
            acc = _dot(vt_ref[:, n0:n0 + tq], jnp.exp2(s_d - m).astype(BF16))
        r = acc[:DA_V_DIM] / acc[DA_V_DIM:DA_V_DIM + 1]
        o = r[:, :tq] - lam * r[:, tq:]
        o = o * lax.rsqrt(jnp.mean(o * o, axis=0, keepdims=True) + LN_EPS) * sub_g
        o_ref[0, n0:n0 + tq, :] = o.T.astype(BF16)

    nb = S // tq
    pending = scores(0)
    for i in range(nb):
        upcoming = scores(i + 1) if i + 1 < nb else None
        finish(i, *pending)
        pending = upcoming


def _da_attn(q, k, v, lam_q, lam_k, subln_g, j, lam_init):
    B, S, _ = q.shape
    head = pl.BlockSpec((1, S, DA_V_DIM), lambda bi, h: (bi, 0, h))
    return pl.pallas_call(
        functools.partial(_da_attn_kernel, lam_init=lam_init),
        grid=(B, DA_HEADS),
        in_specs=[head, head, head, _resident((2, DA_HEAD_DIM), (j,)),
                  _resident((2, DA_HEAD_DIM), (j,)), _resident((DA_V_DIM, 1), (j,))],
        out_specs=head,
        out_shape=jax.ShapeDtypeStruct((B, S, DA_WIDTH), BF16),
        scratch_shapes=[pltpu.VMEM((DA_V_DIM + DA_ONES_ROWS, S), BF16)],
        compiler_params=_params(2),
        name="da_attn",
    )(q, k, v, lam_q, lam_k, subln_g)


POOL_ROWS = 512
POOL_SKIP = 8
assert POOL_WINDOWS == tuple(2 ** (l + 1) for l in range(len(POOL_WINDOWS)))
assert POOL_HALO >= POOL_SKIP + sum(w // 2 for w in POOL_WINDOWS) and POOL_HALO % 8 == 0


def _pool_ln_kernel(x_ref, prev_ref, w_ref, bias_ref, scale_ref, g_ref, b_ref, o_ref, *bufs,
                    tiles_per_seq):
    i = pl.program_id(0)
    t0 = (i % tiles_per_seq) * POOL_ROWS
    x = x_ref[...]
    first = (i % tiles_per_seq) == 0
    n_rows = POOL_HALO + POOL_ROWS
    bufs[0][0:POOL_HALO, :] = jnp.where(first, 0.0, prev_ref[...])
    bufs[0][POOL_HALO:, :] = x
    t = t0 + lax.broadcasted_iota(jnp.int32, (POOL_ROWS, 1), 0)
    outs = []
    for level, w in enumerate(POOL_WINDOWS):
        src = bufs[level]
        shift = w // 2
        tot = src[POOL_SKIP:n_rows, :] + src[POOL_SKIP - shift:n_rows - shift, :]
        if level + 1 < len(POOL_WINDOWS):
            bufs[level + 1][0:POOL_SKIP, :] = jnp.zeros((POOL_SKIP, tot.shape[1] - POOL_GROUP_DIM), F32)
            bufs[level + 1][POOL_SKIP:, :] = tot[:, POOL_GROUP_DIM:]
        xg = x[:, level * POOL_GROUP_DIM:(level + 1) * POOL_GROUP_DIM]
        cnt = jnp.minimum(t + 1, w).astype(F32)
        pooled = tot[POOL_HALO - POOL_SKIP:, 0:POOL_GROUP_DIM] / cnt - xg
        outs.append(_dot(pooled.astype(BF16), w_ref[level]))
    y = (jnp.concatenate(outs, axis=-1) + bias_ref[...]) * scale_ref[...]
    o_ref[...] = _layer_norm(ALPHA * x + y, g_ref[...], b_ref[...])


def _pool_ln(x2d, seq_len, w, bias_all, scale_all, j, ln_g, ln_b, layer):
    T = x2d.shape[0]
    tiles_per_seq = seq_len // POOL_ROWS
    halo_blocks = POOL_ROWS // POOL_HALO
    return pl.pallas_call(
        functools.partial(_pool_ln_kernel, tiles_per_seq=tiles_per_seq),
        grid=(T // POOL_ROWS,),
        in_specs=[pl.BlockSpec((POOL_ROWS, D_MODEL), lambda i: (i, 0)),
                  pl.BlockSpec((POOL_HALO, D_MODEL),
                               lambda i: (jnp.maximum(i * halo_blocks - 1, 0), 0)),
                  _resident(w.shape),
                  _resident((1, D_MODEL), (j,)), _resident((1, D_MODEL), (j,)),
                  _resident((1, D_MODEL), (layer, 1)), _resident((1, D_MODEL), (layer, 1))],
        out_specs=pl.BlockSpec((POOL_ROWS, D_MODEL), lambda i: (i, 0)),
        out_shape=jax.ShapeDtypeStruct((T, D_MODEL), F32),
        scratch_shapes=[pltpu.VMEM((POOL_HALO + POOL_ROWS, D_MODEL - l * POOL_GROUP_DIM), F32)
                        for l in range(len(POOL_WINDOWS))],
        compiler_params=_params(1),
        name="pool_ln",
    )(x2d, x2d, w, bias_all, scale_all, ln_g, ln_b)


RET_PROJ_ROWS = 512
RET_QK_WIDTH = RET_HEADS * RET_QK_DIM
RET_V_WIDTH = RET_HEADS * RET_V_DIM
RET_STEP_ROWS = 512


def _ret_proj_kernel(x_ref, w_ref, cos_ref, sin_ref, q_ref, k_ref, v_ref, sg_ref):
    xb = x_ref[0].astype(BF16)
    cos = cos_ref[0]
    sin = sin_ref[0]
    half = RET_QK_DIM // 2
    for part, out_ref, scale in ((0, q_ref, 1.0), (1, k_ref, RET_QK_DIM ** -0.5)):
        y = _dot(xb, w_ref[:, part * RET_QK_WIDTH:(part + 1) * RET_QK_WIDTH])
        for h in range(RET_HEADS):
            x1 = y[:, h * RET_QK_DIM:h * RET_QK_DIM + half]
            x2 = y[:, h * RET_QK_DIM + half:(h + 1) * RET_QK_DIM]
            out_ref[0, :, h * RET_QK_DIM:h * RET_QK_DIM + half] = (
                (x1 * cos - x2 * sin) * scale).astype(BF16)
            out_ref[0, :, h * RET_QK_DIM + half:(h + 1) * RET_QK_DIM] = (
                (x2 * cos + x1 * sin) * scale).astype(BF16)
    v0 = 2 * RET_QK_WIDTH
    g0 = v0 + RET_V_WIDTH
    for c in range(0, RET_V_WIDTH, 1024):
        v_ref[0, :, c:c + 1024] = _dot(xb, w_ref[:, v0 + c:v0 + c + 1024]).astype(BF16)
        gate = _dot(xb, w_ref[:, g0 + c:g0 + c + 1024])
        sg_ref[0, :, c:c + 1024] = (gate * jax.nn.sigmoid(gate)).astype(BF16)


def _ret_proj(x, w, cos_t, sin_t):
    B, S, _ = x.shape
    grid_row = lambda width: pl.BlockSpec((1, RET_PROJ_ROWS, width), lambda bi, i: (bi, i, 0))
    qk = jax.ShapeDtypeStruct((B, S, RET_QK_WIDTH), BF16)
    vg = jax.ShapeDtypeStruct((B, S, RET_V_WIDTH), BF16)
    return pl.pallas_call(
        _ret_proj_kernel,
        grid=(B, S // RET_PROJ_ROWS),
        in_specs=[grid_row(D_MODEL), _resident(w.shape),
                  grid_row(LANES), grid_row(LANES)],
        out_specs=[grid_row(RET_QK_WIDTH), grid_row(RET_QK_WIDTH),
                   grid_row(RET_V_WIDTH), grid_row(RET_V_WIDTH)],
        out_shape=[qk, qk, vg, vg],
        compiler_params=_params(2),
        name="ret_proj",
    )(x, w, cos_t, sin_t)


def _ret_core_kernel(q_ref, k_ref, v_ref, sg_ref, o_ref, state_ref):
    C = RET_CHUNK

    @pl.when(pl.program_id(1) == 0)
    def _():
        state_ref[...] = jnp.zeros_like(state_ref)

    ri = lax.broadcasted_iota(jnp.int32, (C, C), 0)
    ci = lax.broadcasted_iota(jnp.int32, (C, C), 1)
    rel = (ri - ci).astype(F32)
    idx = lax.broadcasted_iota(jnp.int32, (C, 1), 0).astype(F32)
    decays = []
    for h in range(RET_HEADS):
        log_gamma = math.log(1.0 - 2.0 ** (-5.0 - h))
        decays.append((jnp.where(rel >= 0, jnp.exp(jnp.maximum(rel, 0.0) * log_gamma), 0.0),
                       jnp.exp((idx + 1.0) * log_gamma),
                       jnp.exp((C - 1.0 - idx) * log_gamma),
                       math.exp(C * log_gamma)))
    for c in range(RET_STEP_ROWS // C):
        rows = slice(c * C, (c + 1) * C)
        for h in range(RET_HEADS):
            d_intra, q_decay, k_decay, chunk_decay = decays[h]
            qs = slice(h * RET_QK_DIM, (h + 1) * RET_QK_DIM)
            vs = slice(h * RET_V_DIM, (h + 1) * RET_V_DIM)
            qc = q_ref[0, rows, qs]
            kc = k_ref[0, rows, qs]
            vc = v_ref[0, rows, vs]
            state = state_ref[h]
            att = _dot_nt(qc, kc) * d_intra
            inner = _dot(att.astype(BF16), vc)
            cross = _dot(qc, state.astype(BF16)) * q_decay
            kd = (kc.astype(F32) * k_decay).astype(BF16)
            state_ref[h] = state * chunk_decay + _dot_tn(kd, vc)
            o = inner + cross
            mu = jnp.mean(o, axis=-1, keepdims=True)
            d = o - mu
            var = jnp.mean(d * d, axis=-1, keepdims=True)
            on = d * lax.rsqrt(var + LN_EPS)
            o_ref[0, rows, vs] = (sg_ref[0, rows, vs].astype(F32) * on).astype(BF16)


def _ret_core(q, k, v, sg):
    B, S, _ = q.shape
    blk = lambda width: pl.BlockSpec((1, RET_STEP_ROWS, width), lambda bi, i: (bi, i, 0))
    return pl.pallas_call(
        _ret_core_kernel,
        grid=(B, S // RET_STEP_ROWS),
        in_specs=[blk(RET_QK_WIDTH), blk(RET_QK_WIDTH), blk(RET_V_WIDTH), blk(RET_V_WIDTH)],
        out_specs=blk(RET_V_WIDTH),
        out_shape=jax.ShapeDtypeStruct((B, S, RET_V_WIDTH), BF16),
        scratch_shapes=[pltpu.VMEM((RET_HEADS, RET_QK_DIM, RET_V_DIM), F32)],
        compiler_params=_params(2),
        name="ret_core",
    )(q, k, v, sg)


def _da_rope_tables(pos):
    inv = 1.0 / (ROPE_THETA ** (jnp.arange(0, DA_HEAD_DIM, 2, dtype=F32) / DA_HEAD_DIM))
    ang = pos * inv[None, None, :]
    cos, sin = jnp.cos(ang), jnp.sin(ang)
    return (jnp.concatenate([cos, cos, cos, cos], axis=-1),
            jnp.concatenate([-sin, sin, -sin, sin], axis=-1))


def _ret_rope_tables(pos):
    inv = 1.0 / (ROPE_THETA ** jnp.linspace(0.0, 1.0, RET_QK_DIM // 2, dtype=F32))
    ang = pos * inv[None, None, :]
    return jnp.cos(ang), jnp.sin(ang)


def kernel(x, mem, positions, ffn_w_in, ffn_w_out, ln_g, ln_b, da_w_qkv, da_w_o, da_lam_q, da_lam_k, da_subln_g, pool_w, pool_b, pool_scale, ret_w_qkvg, ret_w_o, xa_wq, xa_wkv, xa_wo):
    B, S, D = x.shape
    T = B * S
    pos = positions.astype(F32)[..., None]
    da_cos, da_sin = _da_rope_tables(pos)
    ret_cos, ret_sin = _ret_rope_tables(pos)
    kv_all = _mem_kv(mem.reshape(-1, D), xa_wkv).reshape(DEPTH, B, mem.shape[1], 2 * D)
    ln_g = ln_g.reshape(DEPTH, 4, 1, D)
    ln_b = ln_b.reshape(DEPTH, 4, 1, D)
    pool_w = pool_w.reshape(-1, D, POOL_GROUP_DIM)
    pool_b = pool_b.reshape(-1, 1, D)
    pool_scale = pool_scale.reshape(-1, 1, D)
    da_subln_g = da_subln_g.reshape(-1, DA_V_DIM, 1)

    def layer_weights(i):
        m, j = i % N_MIXERS, i // N_MIXERS
        mixer = ([("qkv", da_w_qkv, (j,)), ("o", da_w_o, (j,))] if m == 0 else
                 [("pool", pool_w, (j,))] if m == 1 else
                 [("qkvg", ret_w_qkvg, (j,)), ("o", ret_w_o, (j,))])
        return mixer + [("xa_wq", xa_wq, (i,)), ("xa_wo", xa_wo, (i,)),
                        ("w_in", ffn_w_in, (i, 1)), ("w_out", ffn_w_out, (i, 1))]

    def ffn(x, w, i, ln_idx, items):
        y, cast = _ffn_ln(x, w["w_in"], w["w_out"], ln_g, ln_b, i, ln_idx,
                          [(arr, lead) for _, arr, lead in items])
        return y, {name: c for (name, _, _), c in zip(items, cast)}

    w = {"w_in": ffn_w_in[0, 0].astype(BF16), "w_out": ffn_w_out[0, 0].astype(BF16)}
    x = x.reshape(T, D)
    for i in range(DEPTH):
        m, j = i % N_MIXERS, i // N_MIXERS
        x, w = ffn(x, w, i, 0, layer_weights(i))
        if m == 0:
            lam_init = 0.8 - 0.6 * math.exp(-0.3 * i)
            q, k, v = _da_proj(x.reshape(B, S, D), w["qkv"], da_cos, da_sin)
            o = _da_attn(q, k, v, da_lam_q, da_lam_k, da_subln_g, j, lam_init)
            x = _proj_ln(o.reshape(T, -1), w["o"], x, ln_g, ln_b, i)
        elif m == 1:
            pool_w_bf16 = w["pool"].reshape(len(POOL_WINDOWS), POOL_GROUP_DIM, POOL_GROUP_DIM)
            x = _pool_ln(x, S, pool_w_bf16, pool_b, pool_scale, j, ln_g, ln_b, i)
        else:
            q, k, v, sg = _ret_proj(x.reshape(B, S, D), w["qkvg"], ret_cos, ret_sin)
            o = _ret_core(q, k, v, sg)
            x = _proj_ln(o.reshape(T, -1), w["o"], x, ln_g, ln_b, i)
        x = _xattn_ln(x.reshape(B, S, D), kv_all, w["xa_wq"], w["xa_wo"], ln_g, ln_b, i).reshape(T, D)
        nxt = ([("w_in", ffn_w_in, (i + 1, 0)), ("w_out", ffn_w_out, (i + 1, 0))]
               if i + 1 < DEPTH else [])
        x, w = ffn(x, w, i, 3, nxt)
    return x.reshape(B, S, D)
```

```python
import functools
import math

import jax
import jax.numpy as jnp
from jax import lax
from jax.experimental import pallas as pl
from jax.experimental.pallas import tpu as pltpu

D_MODEL = 1024
DEPTH = 4
N_MIXERS = 3
DA_HEADS = 8
DA_HEAD_DIM = 64
DA_V_DIM = 2 * DA_HEAD_DIM
ROPE_THETA = 10000.0
POOL_WINDOWS = (2, 4, 8, 16)
POOL_GROUP_DIM = D_MODEL // len(POOL_WINDOWS)
POOL_HALO = 32
RET_HEADS = 4
RET_QK_DIM = D_MODEL // RET_HEADS
RET_V_DIM = 2 * RET_QK_DIM
RET_CHUNK = 128
XA_HEADS = 4
XA_HEAD_DIM = D_MODEL // XA_HEADS
D_FF = 2816
ALPHA = (2 * DEPTH) ** 0.25
LN_EPS = 1e-5
LOG2E = math.log2(math.e)

BF16 = jnp.bfloat16
F32 = jnp.float32

V7X_VMEM_LIMIT_BYTES = 56 * 1024 * 1024
LANES = 128


def _params(n_axes):
    return pltpu.CompilerParams(
        dimension_semantics=("arbitrary",) * n_axes,
        vmem_limit_bytes=V7X_VMEM_LIMIT_BYTES)


def _resident(tail, lead=()):
    shape = (None,) * len(lead) + tuple(tail)
    index = tuple(lead) + (0,) * len(tail)
    return pl.BlockSpec(shape, lambda *_: index, pipeline_mode=pl.Buffered(1))


def _layer_norm(y, g, b):
    mu = jnp.mean(y, axis=-1, keepdims=True)
    d = y - mu
    var = jnp.mean(d * d, axis=-1, keepdims=True)
    return d * lax.rsqrt(var + LN_EPS) * g + b


def _dot(a, b):
    return jnp.dot(a, b, preferred_element_type=F32)


def _dot_nt(a, b):
    return lax.dot_general(a, b, (((1,), (1,)), ((), ())), preferred_element_type=F32)


def _dot_tn(a, b):
    return lax.dot_general(a, b, (((0,), (0,)), ((), ())), preferred_element_type=F32)


def _matmul_kernel(a_ref, w_ref, o_ref):
    o_ref[...] = _dot(a_ref[...].astype(BF16), w_ref[...].astype(BF16)).astype(o_ref.dtype)


def _mem_kv(mem2d, wkv):
    R = mem2d.shape[0]
    L, _, N = wkv.shape
    return pl.pallas_call(
        _matmul_kernel,
        grid=(L,),
        in_specs=[_resident((R, D_MODEL)),
                  pl.BlockSpec((None, D_MODEL, N), lambda l: (l, 0, 0))],
        out_specs=pl.BlockSpec((None, R, N), lambda l: (l, 0, 0)),
        out_shape=jax.ShapeDtypeStruct((L, R, N), BF16),
        compiler_params=_params(1),
        name="mem_kv",
    )(mem2d, wkv)


FFN_ROWS = 512
FFN_CHUNK = 512


BF16_SUBLANES = 16


def _cast_slab_rows(rows, steps):
    return min(r for r in range(BF16_SUBLANES, rows + 1, BF16_SUBLANES)
               if rows % r == 0 and rows // r <= steps)


def _ffn_ln_kernel(*refs, n_cast):
    x_ref, win_ref, wout_ref, g_ref, b_ref = refs[:5]
    o_ref = refs[5 + n_cast]
    for src_ref, dst_ref in zip(refs[5:5 + n_cast], refs[6 + n_cast:]):
        dst_ref[...] = src_ref[...].astype(BF16)
    x = x_ref[...]
    xb = x.astype(BF16)
    acc = jnp.zeros(x.shape, F32)
    for c0 in range(0, D_FF, FFN_CHUNK):
        c1 = min(c0 + FFN_CHUNK, D_FF)
        hg = _dot(xb, win_ref[:, c0:c1])
        hu = _dot(xb, win_ref[:, D_FF + c0:D_FF + c1])
        act = (hg * jax.nn.sigmoid(hg) * hu).astype(BF16)
        acc = acc + _dot(act, wout_ref[c0:c1, :])
    o_ref[...] = _layer_norm(ALPHA * x + 0.5 * acc, g_ref[...], b_ref[...])


def _ffn_ln(x2d, w_in, w_out, ln_g, ln_b, layer, ln_idx, to_cast):
    T = x2d.shape[0]
    steps = T // FFN_ROWS
    row = pl.BlockSpec((FFN_ROWS, D_MODEL), lambda i: (i, 0))
    cast_specs_in, cast_specs_out, cast_shapes = [], [], []
    for arr, lead in to_cast:
        rows, cols = arr.shape[-2:]
        slab = _cast_slab_rows(rows, steps)
        last = rows // slab - 1
        lead = tuple(lead)
        cast_specs_in.append(pl.BlockSpec(
            (None,) * len(lead) + (slab, cols),
            lambda i, lead=lead, last=last: lead + (jnp.minimum(i, last), 0)))
        cast_specs_out.append(pl.BlockSpec((slab, cols),
                                           lambda i, last=last: (jnp.minimum(i, last), 0)))
        cast_shapes.append(jax.ShapeDtypeStruct((rows, cols), BF16))
    outs = pl.pallas_call(
        functools.partial(_ffn_ln_kernel, n_cast=len(to_cast)),
        grid=(steps,),
        in_specs=[row,
                  _resident((D_MODEL, 2 * D_FF)),
                  _resident((D_FF, D_MODEL)),
                  _resident((1, D_MODEL), (layer, ln_idx)),
                  _resident((1, D_MODEL), (layer, ln_idx))] + cast_specs_in,
        out_specs=[row] + cast_specs_out,
        out_shape=[jax.ShapeDtypeStruct((T, D_MODEL), F32)] + cast_shapes,
        compiler_params=_params(1),
        name="ffn_ln",
    )(x2d, w_in, w_out, ln_g, ln_b, *[arr for arr, _ in to_cast])
    return outs[0], list(outs[1:])


XA_ROWS = 1024
XA_SUB = 256


def _xattn_ln_kernel(*refs, mixer_proj):
    x_ref = refs[0]
    kv_ref, wq_ref, wo_ref, g_ref, b_ref, o_ref = refs[-6:]

    def project(r0):
        x = x_ref[0, r0:r0 + XA_SUB, :]
        if mixer_proj:
            a_ref, wm_ref, gm_ref, bm_ref = refs[1:5]
            x = _layer_norm(ALPHA * x + _dot(a_ref[0, r0:r0 + XA_SUB, :], wm_ref[...]),
                            gm_ref[...], bm_ref[...])
        return x, (_dot(x.astype(BF16), wq_ref[...]) * (XA_HEAD_DIM ** -0.5 * LOG2E)).astype(BF16)

    def finish(r0, x, h_out):
        o_ref[0, r0:r0 + XA_SUB, :] = _layer_norm(ALPHA * x + h_out, g_ref[...], b_ref[...])

    starts = list(range(0, XA_ROWS, XA_SUB))
    x, q = project(starts[0])
    lagged = None
    for n, r0 in enumerate(starts):
        s = [_dot_nt(q[:, h * XA_HEAD_DIM:(h + 1) * XA_HEAD_DIM],
                     kv_ref[:, h * XA_HEAD_DIM:(h + 1) * XA_HEAD_DIM]) for h in range(XA_HEADS)]
        upcoming = project(starts[n + 1]) if n + 1 < len(starts) else None
        heads = []
        for h in range(XA_HEADS):
            e = jnp.exp2(s[h] - jnp.max(s[h], axis=-1, keepdims=True))
            p = e * (1.0 / jnp.sum(e, axis=-1, keepdims=True))
            lo = D_MODEL + h * XA_HEAD_DIM
            heads.append(_dot(p.astype(BF16), kv_ref[:, lo:lo + XA_HEAD_DIM]).astype(BF16))
        if lagged is not None:
            finish(*lagged)
        lagged = (r0, x, _dot(jnp.concatenate(heads, axis=-1), wo_ref[...]))
        if upcoming is not None:
            x, q = upcoming
    finish(*lagged)


def _xattn_ln(x, kv_all, wq, wo, ln_g, ln_b, layer, mixer=None):
    B, S, _ = x.shape
    M = kv_all.shape[2]
    row = pl.BlockSpec((1, XA_ROWS, D_MODEL), lambda bi, i: (bi, i, 0))
    mixer_specs, mixer_args = [], []
    if mixer is not None:
        a, w_mix = mixer
        K = a.shape[-1]
        mixer_specs = [pl.BlockSpec((1, XA_ROWS, K), lambda bi, i: (bi, i, 0)),
                       _resident((K, D_MODEL)),
                       _resident((1, D_MODEL), (layer, 1)), _resident((1, D_MODEL), (layer, 1))]
        mixer_args = [a, w_mix, ln_g, ln_b]
    return pl.pallas_call(
        functools.partial(_xattn_ln_kernel, mixer_proj=mixer is not None),
        grid=(B, S // XA_ROWS),
        in_specs=[row] + mixer_specs + [
            pl.BlockSpec((None, None, M, 2 * D_MODEL), lambda bi, i: (layer, bi, 0, 0)),
            _resident((D_MODEL, D_MODEL)), _resident((D_MODEL, D_MODEL)),
            _resident((1, D_MODEL), (layer, 2)), _resident((1, D_MODEL), (layer, 2))],
        out_specs=row,
        out_shape=jax.ShapeDtypeStruct((B, S, D_MODEL), F32),
        compiler_params=_params(2),
        name="xattn_ln",
    )(x, *mixer_args, kv_all, wq, wo, ln_g, ln_b)


DA_PROJ_ROWS = 512
DA_QBLOCK = 256
DA_WIDTH = DA_HEADS * DA_V_DIM


def _da_proj_kernel(x_ref, w_ref, cos_ref, sin_ref, q_ref, k_ref, v_ref):
    xb = x_ref[0].astype(BF16)
    cos = cos_ref[0]
    sin = sin_ref[0]
    lane = lax.broadcasted_iota(jnp.int32, (1, DA_V_DIM), 1)
    first_half = (lane % DA_HEAD_DIM) < (DA_HEAD_DIM // 2)
    q_scale = DA_HEAD_DIM ** -0.5 * LOG2E
    for part, out_ref, scale in ((0, q_ref, q_scale), (1, k_ref, 1.0)):
        y = _dot(xb, w_ref[:, part * DA_WIDTH:(part + 1) * DA_WIDTH])
        for h in range(DA_HEADS):
            yh = y[:, h * DA_V_DIM:(h + 1) * DA_V_DIM]
            partner = jnp.where(first_half,
                                pltpu.roll(yh, DA_V_DIM - DA_HEAD_DIM // 2, axis=1),
                                pltpu.roll(yh, DA_HEAD_DIM // 2, axis=1))
            r = yh * cos + partner * sin
            out_ref[0, :, h * DA_V_DIM:(h + 1) * DA_V_DIM] = (r * scale).astype(BF16)
    v_ref[0] = _dot(xb, w_ref[:, 2 * DA_WIDTH:]).astype(BF16)


def _da_proj(x, w_qkv, cos_t, sin_t):
    B, S, _ = x.shape
    row = pl.BlockSpec((1, DA_PROJ_ROWS, D_MODEL), lambda bi, i: (bi, i, 0))
    tab = pl.BlockSpec((1, DA_PROJ_ROWS, LANES), lambda bi, i: (bi, i, 0))
    out = jax.ShapeDtypeStruct((B, S, DA_WIDTH), BF16)
    return pl.pallas_call(
        _da_proj_kernel,
        grid=(B, S // DA_PROJ_ROWS),
        in_specs=[row, _resident((D_MODEL, 3 * DA_WIDTH)), tab, tab],
        out_specs=[row, row, row],
        out_shape=[out, out, out],
        compiler_params=_params(2),
        name="da_proj",
    )(x, w_qkv, cos_t, sin_t)


DA_ONES_ROWS = 16


def _da_attn_kernel(q_ref, k_ref, v_ref, lq_ref, lk_ref, sg_ref, o_ref, vt_ref, *, lam_init):
    S = q_ref.shape[1]
    tq = DA_QBLOCK
    vt_ref[:DA_V_DIM, :] = v_ref[0].astype(F32).T.astype(BF16)
    vt_ref[DA_V_DIM:, :] = jnp.ones((DA_ONES_ROWS, S), BF16)
    lqk = lq_ref[...] * lk_ref[...]
    lam = (jnp.exp(jnp.sum(lqk[0:1], axis=-1, keepdims=True))
           - jnp.exp(jnp.sum(lqk[1:2], axis=-1, keepdims=True)) + lam_init)
    lane = lax.broadcasted_iota(jnp.int32, (1, DA_V_DIM), 1)
    map0 = lane < DA_HEAD_DIM
    sub_g = sg_ref[...] * (1.0 - lam_init)
    key = lax.broadcasted_iota(jnp.int32, (tq, 2 * tq), 0)
    qry = lax.broadcasted_iota(jnp.int32, (tq, 2 * tq), 1)
    causal = key <= jnp.where(qry >= tq, qry - tq, qry)
    neg = jnp.finfo(F32).min

    def scores(i):
        n0 = i * tq
        q = q_ref[0, n0:n0 + tq, :]
        zero = jnp.zeros_like(q)
        q2 = jnp.concatenate([jnp.where(map0, q, zero), jnp.where(map0, zero, q)], axis=0)
        s_d = jnp.where(causal, _dot_nt(k_ref[0, n0:n0 + tq, :], q2), neg)
        s_p = _dot_nt(k_ref[0, :n0, :], q2) if i > 0 else None
        return s_p, s_d

    def finish(i, s_p, s_d):
        n0 = i * tq
        m = jnp.max(s_d, axis=0, keepdims=True)
        if i > 0:
            m = jnp.maximum(m, jnp.max(s_p, axis=0, keepdims=True))
            acc = _dot(vt_ref[:, :n0], jnp.exp2(s_p - m).astype(BF16))
            acc = acc + _dot(vt_ref[:, n0:n0 + tq], jnp.exp2(s_d - m).astype(BF16))
        else:
            acc = _dot(vt_ref[:, n0:n0 + tq], jnp.exp2(s_d - m).astype(BF16))
        r = acc[:DA_V_DIM] / acc[DA_V_DIM:DA_V_DIM + 1]
        o = r[:, :tq] - lam * r[:, tq:]
        o = o * lax.rsqrt(jnp.mean(o * o, axis=0, keepdims=True) + LN_EPS) * sub_g
        o_ref[0, n0:n0 + tq, :] = o.T.astype(BF16)

    nb = S // tq
    pending = scores(0)
    for i in range(nb):
        upcoming = scores(i + 1) if i + 1 < nb else None
        finish(i, *pending)
        pending = upcoming


def _da_attn(q, k, v, lam_q, lam_k, subln_g, j, lam_init):
    B, S, _ = q.shape
    head = pl.BlockSpec((1, S, DA_V_DIM), lambda bi, h: (bi, 0, h))
    return pl.pallas_call(
        functools.partial(_da_attn_kernel, lam_init=lam_init),
        grid=(B, DA_HEADS),
        in_specs=[head, head, head, _resident((2, DA_HEAD_DIM), (j,)),
                  _resident((2, DA_HEAD_DIM), (j,)), _resident((DA_V_DIM, 1), (j,))],
        out_specs=head,
        out_shape=jax.ShapeDtypeStruct((B, S, DA_WIDTH), BF16),
        scratch_shapes=[pltpu.VMEM((DA_V_DIM + DA_ONES_ROWS, S), BF16)],
        compiler_params=_params(2),
        name="da_attn",
    )(q, k, v, lam_q, lam_k, subln_g)


POOL_ROWS = 512
POOL_SKIP = 8
assert POOL_WINDOWS == tuple(2 ** (l + 1) for l in range(len(POOL_WINDOWS)))
assert POOL_HALO >= POOL_SKIP + sum(w // 2 for w in POOL_WINDOWS) and POOL_HALO % 8 == 0


def _pool_ln_kernel(x_ref, prev_ref, w_ref, bias_ref, scale_ref, g_ref, b_ref, o_ref, *bufs,
                    tiles_per_seq):
    i = pl.program_id(0)
    t0 = (i % tiles_per_seq) * POOL_ROWS
    x = x_ref[...]
    first = (i % tiles_per_seq) == 0
    n_rows = POOL_HALO + POOL_ROWS
    bufs[0][0:POOL_HALO, :] = jnp.where(first, 0.0, prev_ref[...])
    bufs[0][POOL_HALO:, :] = x
    t = t0 + lax.broadcasted_iota(jnp.int32, (POOL_ROWS, 1), 0)
    outs = []
    for level, w in enumerate(POOL_WINDOWS):
        src = bufs[level]
        shift = w // 2
        tot = src[POOL_SKIP:n_rows, :] + src[POOL_SKIP - shift:n_rows - shift, :]
        if level + 1 < len(POOL_WINDOWS):
            bufs[level + 1][0:POOL_SKIP, :] = jnp.zeros((POOL_SKIP, tot.shape[1] - POOL_GROUP_DIM), F32)
            bufs[level + 1][POOL_SKIP:, :] = tot[:, POOL_GROUP_DIM:]
        xg = x[:, level * POOL_GROUP_DIM:(level + 1) * POOL_GROUP_DIM]
        cnt = jnp.minimum(t + 1, w).astype(F32)
        pooled = tot[POOL_HALO - POOL_SKIP:, 0:POOL_GROUP_DIM] / cnt - xg
        outs.append(_dot(pooled.astype(BF16), w_ref[level]))
    y = (jnp.concatenate(outs, axis=-1) + bias_ref[...]) * scale_ref[...]
    o_ref[...] = _layer_norm(ALPHA * x + y, g_ref[...], b_ref[...])


def _pool_ln(x2d, seq_len, w, bias_all, scale_all, j, ln_g, ln_b, layer):
    T = x2d.shape[0]
    tiles_per_seq = seq_len // POOL_ROWS
    halo_blocks = POOL_ROWS // POOL_HALO
    return pl.pallas_call(
        functools.partial(_pool_ln_kernel, tiles_per_seq=tiles_per_seq),
        grid=(T // POOL_ROWS,),
        in_specs=[pl.BlockSpec((POOL_ROWS, D_MODEL), lambda i: (i, 0)),
                  pl.BlockSpec((POOL_HALO, D_MODEL),
                               lambda i: (jnp.maximum(i * halo_blocks - 1, 0), 0)),
                  _resident(w.shape),
                  _resident((1, D_MODEL), (j,)), _resident((1, D_MODEL), (j,)),
                  _resident((1, D_MODEL), (layer, 1)), _resident((1, D_MODEL), (layer, 1))],
        out_specs=pl.BlockSpec((POOL_ROWS, D_MODEL), lambda i: (i, 0)),
        out_shape=jax.ShapeDtypeStruct((T, D_MODEL), F32),
        scratch_shapes=[pltpu.VMEM((POOL_HALO + POOL_ROWS, D_MODEL - l * POOL_GROUP_DIM), F32)
                        for l in range(len(POOL_WINDOWS))],
        compiler_params=_params(1),
        name="pool_ln",
    )(x2d, x2d, w, bias_all, scale_all, ln_g, ln_b)


RET_PROJ_ROWS = 512
RET_QK_WIDTH = RET_HEADS * RET_QK_DIM
RET_V_WIDTH = RET_HEADS * RET_V_DIM
RET_STEP_ROWS = 512


def _ret_proj_kernel(x_ref, w_ref, cos_ref, sin_ref, q_ref, k_ref, v_ref, sg_ref):
    xb = x_ref[0].astype(BF16)
    cos = cos_ref[0]
    sin = sin_ref[0]
    half = RET_QK_DIM // 2
    for part, out_ref, scale in ((0, q_ref, 1.0), (1, k_ref, RET_QK_DIM ** -0.5)):
        y = _dot(xb, w_ref[:, part * RET_QK_WIDTH:(part + 1) * RET_QK_WIDTH])
        for h in range(RET_HEADS):
            x1 = y[:, h * RET_QK_DIM:h * RET_QK_DIM + half]
            x2 = y[:, h * RET_QK_DIM + half:(h + 1) * RET_QK_DIM]
            out_ref[0, :, h * RET_QK_DIM:h * RET_QK_DIM + half] = (
                (x1 * cos - x2 * sin) * scale).astype(BF16)
            out_ref[0, :, h * RET_QK_DIM + half:(h + 1) * RET_QK_DIM] = (
                (x2 * cos + x1 * sin) * scale).astype(BF16)
    v0 = 2 * RET_QK_WIDTH
    g0 = v0 + RET_V_WIDTH
    for c in range(0, RET_V_WIDTH, 1024):
        v_ref[0, :, c:c + 1024] = _dot(xb, w_ref[:, v0 + c:v0 + c + 1024]).astype(BF16)
        gate = _dot(xb, w_ref[:, g0 + c:g0 + c + 1024])
        sg_ref[0, :, c:c + 1024] = (gate * jax.nn.sigmoid(gate)).astype(BF16)


def _ret_proj(x, w, cos_t, sin_t):
    B, S, _ = x.shape
    grid_row = lambda width: pl.BlockSpec((1, RET_PROJ_ROWS, width), lambda bi, i: (bi, i, 0))
    qk = jax.ShapeDtypeStruct((B, S, RET_QK_WIDTH), BF16)
    vg = jax.ShapeDtypeStruct((B, S, RET_V_WIDTH), BF16)
    return pl.pallas_call(
        _ret_proj_kernel,
        grid=(B, S // RET_PROJ_ROWS),
        in_specs=[grid_row(D_MODEL), _resident(w.shape),
                  grid_row(LANES), grid_row(LANES)],
        out_specs=[grid_row(RET_QK_WIDTH), grid_row(RET_QK_WIDTH),
                   grid_row(RET_V_WIDTH), grid_row(RET_V_WIDTH)],
        out_shape=[qk, qk, vg, vg],
        compiler_params=_params(2),
        name="ret_proj",
    )(x, w, cos_t, sin_t)


def _ret_core_kernel(q_ref, k_ref, v_ref, sg_ref, o_ref, state_ref):
    C = RET_CHUNK

    @pl.when(pl.program_id(1) == 0)
    def _():
        state_ref[...] = jnp.zeros_like(state_ref)

    ri = lax.broadcasted_iota(jnp.int32, (C, C), 0)
    ci = lax.broadcasted_iota(jnp.int32, (C, C), 1)
    rel = (ri - ci).astype(F32)
    idx = lax.broadcasted_iota(jnp.int32, (C, 1), 0).astype(F32)
    decays = []
    for h in range(RET_HEADS):
        log_gamma = math.log(1.0 - 2.0 ** (-5.0 - h))
        decays.append((jnp.where(rel >= 0, jnp.exp(jnp.maximum(rel, 0.0) * log_gamma), 0.0),
                       jnp.exp((idx + 1.0) * log_gamma),
                       jnp.exp((C - 1.0 - idx) * log_gamma),
                       math.exp(C * log_gamma)))
    for c in range(RET_STEP_ROWS // C):
        rows = slice(c * C, (c + 1) * C)
        for h in range(RET_HEADS):
            d_intra, q_decay, k_decay, chunk_decay = decays[h]
            qs = slice(h * RET_QK_DIM, (h + 1) * RET_QK_DIM)
            vs = slice(h * RET_V_DIM, (h + 1) * RET_V_DIM)
            qc = q_ref[0, rows, qs]
            kc = k_ref[0, rows, qs]
            vc = v_ref[0, rows, vs]
            state = state_ref[h]
            att = _dot_nt(qc, kc) * d_intra
            inner = _dot(att.astype(BF16), vc)
            cross = _dot(qc, state.astype(BF16)) * q_decay
            kd = (kc.astype(F32) * k_decay).astype(BF16)
            state_ref[h] = state * chunk_decay + _dot_tn(kd, vc)
            o = inner + cross
            mu = jnp.mean(o, axis=-1, keepdims=True)
            d = o - mu
            var = jnp.mean(d * d, axis=-1, keepdims=True)
            on = d * lax.rsqrt(var + LN_EPS)
            o_ref[0, rows, vs] = (sg_ref[0, rows, vs].astype(F32) * on).astype(BF16)


def _ret_core(q, k, v, sg):
    B, S, _ = q.shape
    blk = lambda width: pl.BlockSpec((1, RET_STEP_ROWS, width), lambda bi, i: (bi, i, 0))
    return pl.pallas_call(
        _ret_core_kernel,
        grid=(B, S // RET_STEP_ROWS),
        in_specs=[blk(RET_QK_WIDTH), blk(RET_QK_WIDTH), blk(RET_V_WIDTH), blk(RET_V_WIDTH)],
        out_specs=blk(RET_V_WIDTH),
        out_shape=jax.ShapeDtypeStruct((B, S, RET_V_WIDTH), BF16),
        scratch_shapes=[pltpu.VMEM((RET_HEADS, RET_QK_DIM, RET_V_DIM), F32)],
        compiler_params=_params(2),
        name="ret_core",
    )(q, k, v, sg)


def _da_rope_tables(pos):
    inv = 1.0 / (ROPE_THETA ** (jnp.arange(0, DA_HEAD_DIM, 2, dtype=F32) / DA_HEAD_DIM))
    ang = pos * inv[None, None, :]
    cos, sin = jnp.cos(ang), jnp.sin(ang)
    return (jnp.concatenate([cos, cos, cos, cos], axis=-1),
            jnp.concatenate([-sin, sin, -sin, sin], axis=-1))


def _ret_rope_tables(pos):
    inv = 1.0 / (ROPE_THETA ** jnp.linspace(0.0, 1.0, RET_QK_DIM // 2, dtype=F32))
    ang = pos * inv[None, None, :]
    return jnp.cos(ang), jnp.sin(ang)


def kernel(x, mem, positions, ffn_w_in, ffn_w_out, ln_g, ln_b, da_w_qkv, da_w_o, da_lam_q, da_lam_k, da_subln_g, pool_w, pool_b, pool_scale, ret_w_qkvg, ret_w_o, xa_wq, xa_wkv, xa_wo):
    B, S, D = x.shape
    T = B * S
    pos = positions.astype(F32)[..., None]
    da_cos, da_sin = _da_rope_tables(pos)
    ret_cos, ret_sin = _ret_rope_tables(pos)
    kv_all = _mem_kv(mem.reshape(-1, D), xa_wkv).reshape(DEPTH, B, mem.shape[1], 2 * D)
    ln_g = ln_g.reshape(DEPTH, 4, 1, D)
    ln_b = ln_b.reshape(DEPTH, 4, 1, D)
    pool_w = pool_w.reshape(-1, D, POOL_GROUP_DIM)
    pool_b = pool_b.reshape(-1, 1, D)
    pool_scale = pool_scale.reshape(-1, 1, D)
    da_subln_g = da_subln_g.reshape(-1, DA_V_DIM, 1)

    def layer_weights(i):
        m, j = i % N_MIXERS, i // N_MIXERS
        mixer = ([("qkv", da_w_qkv, (j,)), ("o", da_w_o, (j,))] if m == 0 else
                 [("pool", pool_w, (j,))] if m == 1 else
                 [("qkvg", ret_w_qkvg, (j,)), ("o", ret_w_o, (j,))])
        return mixer + [("xa_wq", xa_wq, (i,)), ("xa_wo", xa_wo, (i,)),
                        ("w_in", ffn_w_in, (i, 1)), ("w_out", ffn_w_out, (i, 1))]

    def ffn(x, w, i, ln_idx, items):
        y, cast = _ffn_ln(x, w["w_in"], w["w_out"], ln_g, ln_b, i, ln_idx,
                          [(arr, lead) for _, arr, lead in items])
        return y, {name: c for (name, _, _), c in zip(items, cast)}

    w = {"w_in": ffn_w_in[0, 0].astype(BF16), "w_out": ffn_w_out[0, 0].astype(BF16)}
    x = x.reshape(T, D)
    for i in range(DEPTH):
        m, j = i % N_MIXERS, i // N_MIXERS
        x, w = ffn(x, w, i, 0, layer_weights(i))
        mixer = None
        if m == 0:
            lam_init = 0.8 - 0.6 * math.exp(-0.3 * i)
            q, k, v = _da_proj(x.reshape(B, S, D), w["qkv"], da_cos, da_sin)
            mixer = (_da_attn(q, k, v, da_lam_q, da_lam_k, da_subln_g, j, lam_init), w["o"])
        elif m == 1:
            pool_w_bf16 = w["pool"].reshape(len(POOL_WINDOWS), POOL_GROUP_DIM, POOL_GROUP_DIM)
            x = _pool_ln(x, S, pool_w_bf16, pool_b, pool_scale, j, ln_g, ln_b, i)
        else:
            q, k, v, sg = _ret_proj(x.reshape(B, S, D), w["qkvg"], ret_cos, ret_sin)
            mixer = (_ret_core(q, k, v, sg), w["o"])
        x = _xattn_ln(x.reshape(B, S, D), kv_all, w["xa_wq"], w["xa_wo"], ln_g, ln_b, i,
                      mixer).reshape(T, D)
        nxt = ([("w_in", ffn_w_in, (i + 1, 0)), ("w_out", ffn_w_out, (i + 1, 0))]
               if i + 1 < DEPTH else [])
        x, w = ffn(x, w, i, 3, nxt)
    return x.reshape(B, S, D)
```

```python
import functools
import math

import jax
import jax.numpy as jnp
from jax import lax
from jax.experimental import pallas as pl
from jax.experimental.pallas import tpu as pltpu

D_MODEL = 1024
DEPTH = 4
N_MIXERS = 3
DA_HEADS = 8
DA_HEAD_DIM = 64
DA_V_DIM = 2 * DA_HEAD_DIM
ROPE_THETA = 10000.0
POOL_WINDOWS = (2, 4, 8, 16)
POOL_GROUP_DIM = D_MODEL // len(POOL_WINDOWS)
POOL_HALO = 32
RET_HEADS = 4
RET_QK_DIM = D_MODEL // RET_HEADS
RET_V_DIM = 2 * RET_QK_DIM
RET_CHUNK = 128
XA_HEADS = 4
XA_HEAD_DIM = D_MODEL // XA_HEADS
D_FF = 2816
ALPHA = (2 * DEPTH) ** 0.25
LN_EPS = 1e-5
LOG2E = math.log2(math.e)

BF16 = jnp.bfloat16
F32 = jnp.float32

V7X_VMEM_LIMIT_BYTES = 56 * 1024 * 1024
LANES = 128


def _params(n_axes):
    return pltpu.CompilerParams(
        dimension_semantics=("arbitrary",) * n_axes,
        vmem_limit_bytes=V7X_VMEM_LIMIT_BYTES)


def _resident(tail, lead=()):
    shape = (None,) * len(lead) + tuple(tail)
    index = tuple(lead) + (0,) * len(tail)
    return pl.BlockSpec(shape, lambda *_: index, pipeline_mode=pl.Buffered(1))


def _layer_norm(y, g, b):
    mu = jnp.mean(y, axis=-1, keepdims=True)
    d = y - mu
    var = jnp.mean(d * d, axis=-1, keepdims=True)
    return d * lax.rsqrt(var + LN_EPS) * g + b


def _dot(a, b):
    return jnp.dot(a, b, preferred_element_type=F32)


def _dot_nt(a, b):
    return lax.dot_general(a, b, (((1,), (1,)), ((), ())), preferred_element_type=F32)


def _dot_tn(a, b):
    return lax.dot_general(a, b, (((0,), (0,)), ((), ())), preferred_element_type=F32)


def _matmul_kernel(a_ref, w_ref, o_ref):
    o_ref[...] = _dot(a_ref[...].astype(BF16), w_ref[...].astype(BF16)).astype(o_ref.dtype)


def _mem_kv(mem2d, wkv):
    R = mem2d.shape[0]
    L, _, N = wkv.shape
    return pl.pallas_call(
        _matmul_kernel,
        grid=(L,),
        in_specs=[_resident((R, D_MODEL)),
                  pl.BlockSpec((None, D_MODEL, N), lambda l: (l, 0, 0))],
        out_specs=pl.BlockSpec((None, R, N), lambda l: (l, 0, 0)),
        out_shape=jax.ShapeDtypeStruct((L, R, N), BF16),
        compiler_params=_params(1),
        name="mem_kv",
    )(mem2d, wkv)


FFN_ROWS = 1024
FFN_CHUNK = 512


BF16_SUBLANES = 16


def _cast_slab_rows(rows, steps):
    return min(r for r in range(BF16_SUBLANES, rows + 1, BF16_SUBLANES)
               if rows % r == 0 and rows // r <= steps)


def _ffn_ln_kernel(*refs, n_cast):
    x_ref, win_ref, wout_ref, g_ref, b_ref = refs[:5]
    o_ref = refs[5 + n_cast]
    for src_ref, dst_ref in zip(refs[5:5 + n_cast], refs[6 + n_cast:]):
        dst_ref[...] = src_ref[...].astype(BF16)
    x = x_ref[...]
    xb = x.astype(BF16)
    acc = jnp.zeros(x.shape, F32)
    for c0 in range(0, D_FF, FFN_CHUNK):
        c1 = min(c0 + FFN_CHUNK, D_FF)
        hg = _dot(xb, win_ref[:, c0:c1])
        hu = _dot(xb, win_ref[:, D_FF + c0:D_FF + c1])
        act = (hg * jax.nn.sigmoid(hg) * hu).astype(BF16)
        acc = acc + _dot(act, wout_ref[c0:c1, :])
    o_ref[...] = _layer_norm(ALPHA * x + 0.5 * acc, g_ref[...], b_ref[...])


def _ffn_ln(x2d, w_in, w_out, ln_g, ln_b, layer, ln_idx, to_cast):
    T = x2d.shape[0]
    steps = T // FFN_ROWS
    row = pl.BlockSpec((FFN_ROWS, D_MODEL), lambda i: (i, 0))
    cast_specs_in, cast_specs_out, cast_shapes = [], [], []
    for arr, lead in to_cast:
        rows, cols = arr.shape[-2:]
        slab = _cast_slab_rows(rows, steps)
        last = rows // slab - 1
        lead = tuple(lead)
        cast_specs_in.append(pl.BlockSpec(
            (None,) * len(lead) + (slab, cols),
            lambda i, lead=lead, last=last: lead + (jnp.minimum(i, last), 0)))
        cast_specs_out.append(pl.BlockSpec((slab, cols),
                                           lambda i, last=last: (jnp.minimum(i, last), 0)))
        cast_shapes.append(jax.ShapeDtypeStruct((rows, cols), BF16))
    outs = pl.pallas_call(
        functools.partial(_ffn_ln_kernel, n_cast=len(to_cast)),
        grid=(steps,),
        in_specs=[row,
                  _resident((D_MODEL, 2 * D_FF)),
                  _resident((D_FF, D_MODEL)),
                  _resident((1, D_MODEL), (layer, ln_idx)),
                  _resident((1, D_MODEL), (layer, ln_idx))] + cast_specs_in,
        out_specs=[row] + cast_specs_out,
        out_shape=[jax.ShapeDtypeStruct((T, D_MODEL), F32)] + cast_shapes,
        compiler_params=_params(1),
        name="ffn_ln",
    )(x2d, w_in, w_out, ln_g, ln_b, *[arr for arr, _ in to_cast])
    return outs[0], list(outs[1:])


XA_ROWS = 1024
XA_SUB = 256


def _xattn_ln_kernel(*refs, mixer_proj):
    x_ref = refs[0]
    kv_ref, wq_ref, wo_ref, g_ref, b_ref, o_ref = refs[-6:]

    def project(r0):
        x = x_ref[0, r0:r0 + XA_SUB, :]
        if mixer_proj:
            a_ref, wm_ref, gm_ref, bm_ref = refs[1:5]
            x = _layer_norm(ALPHA * x + _dot(a_ref[0, r0:r0 + XA_SUB, :], wm_ref[...]),
                            gm_ref[...], bm_ref[...])
        return x, (_dot(x.astype(BF16), wq_ref[...]) * (XA_HEAD_DIM ** -0.5 * LOG2E)).astype(BF16)

    def finish(r0, x, h_out):
        o_ref[0, r0:r0 + XA_SUB, :] = _layer_norm(ALPHA * x + h_out, g_ref[...], b_ref[...])

    starts = list(range(0, XA_ROWS, XA_SUB))
    x, q = project(starts[0])
    lagged = None
    for n, r0 in enumerate(starts):
        s = [_dot_nt(q[:, h * XA_HEAD_DIM:(h + 1) * XA_HEAD_DIM],
                     kv_ref[:, h * XA_HEAD_DIM:(h + 1) * XA_HEAD_DIM]) for h in range(XA_HEADS)]
        upcoming = project(starts[n + 1]) if n + 1 < len(starts) else None
        heads = []
        for h in range(XA_HEADS):
            e = jnp.exp2(s[h] - jnp.max(s[h], axis=-1, keepdims=True))
            p = e * (1.0 / jnp.sum(e, axis=-1, keepdims=True))
            lo = D_MODEL + h * XA_HEAD_DIM
            heads.append(_dot(p.astype(BF16), kv_ref[:, lo:lo + XA_HEAD_DIM]).astype(BF16))
        if lagged is not None:
            finish(*lagged)
        lagged = (r0, x, _dot(jnp.concatenate(heads, axis=-1), wo_ref[...]))
        if upcoming is not None:
            x, q = upcoming
    finish(*lagged)


def _xattn_ln(x, kv_all, wq, wo, ln_g, ln_b, layer, mixer=None):
    B, S, _ = x.shape
    M = kv_all.shape[2]
    row = pl.BlockSpec((1, XA_ROWS, D_MODEL), lambda bi, i: (bi, i, 0))
    mixer_specs, mixer_args = [], []
    if mixer is not None:
        a, w_mix = mixer
        K = a.shape[-1]
        mixer_specs = [pl.BlockSpec((1, XA_ROWS, K), lambda bi, i: (bi, i, 0)),
                       _resident((K, D_MODEL)),
                       _resident((1, D_MODEL), (layer, 1)), _resident((1, D_MODEL), (layer, 1))]
        mixer_args = [a, w_mix, ln_g, ln_b]
    return pl.pallas_call(
        functools.partial(_xattn_ln_kernel, mixer_proj=mixer is not None),
        grid=(B, S // XA_ROWS),
        in_specs=[row] + mixer_specs + [
            pl.BlockSpec((None, None, M, 2 * D_MODEL), lambda bi, i: (layer, bi, 0, 0)),
            _resident((D_MODEL, D_MODEL)), _resident((D_MODEL, D_MODEL)),
            _resident((1, D_MODEL), (layer, 2)), _resident((1, D_MODEL), (layer, 2))],
        out_specs=row,
        out_shape=jax.ShapeDtypeStruct((B, S, D_MODEL), F32),
        compiler_params=_params(2),
        name="xattn_ln",
    )(x, *mixer_args, kv_all, wq, wo, ln_g, ln_b)


DA_PROJ_ROWS = 512
DA_QBLOCK = 512
DA_WIDTH = DA_HEADS * DA_V_DIM


def _da_proj_kernel(x_ref, w_ref, cos_ref, sin_ref, q_ref, k_ref, v_ref):
    xb = x_ref[0].astype(BF16)
    cos = cos_ref[0]
    sin = sin_ref[0]
    lane = lax.broadcasted_iota(jnp.int32, (1, DA_V_DIM), 1)
    first_half = (lane % DA_HEAD_DIM) < (DA_HEAD_DIM // 2)
    q_scale = DA_HEAD_DIM ** -0.5 * LOG2E
    for part, out_ref, scale in ((0, q_ref, q_scale), (1, k_ref, 1.0)):
        y = _dot(xb, w_ref[:, part * DA_WIDTH:(part + 1) * DA_WIDTH])
        for h in range(DA_HEADS):
            yh = y[:, h * DA_V_DIM:(h + 1) * DA_V_DIM]
            partner = jnp.where(first_half,
                                pltpu.roll(yh, DA_V_DIM - DA_HEAD_DIM // 2, axis=1),
                                pltpu.roll(yh, DA_HEAD_DIM // 2, axis=1))
            r = yh * cos + partner * sin
            out_ref[h] = (r * scale).astype(BF16)
    v = _dot(xb, w_ref[:, 2 * DA_WIDTH:]).astype(BF16)
    for h in range(DA_HEADS):
        v_ref[h] = v[:, h * DA_V_DIM:(h + 1) * DA_V_DIM]


def _da_proj(x, w_qkv, cos_t, sin_t):
    B, S, _ = x.shape
    row = pl.BlockSpec((1, DA_PROJ_ROWS, D_MODEL), lambda bi, i: (bi, i, 0))
    tab = pl.BlockSpec((1, DA_PROJ_ROWS, LANES), lambda bi, i: (bi, i, 0))
    heads = pl.BlockSpec((None, DA_HEADS, DA_PROJ_ROWS, DA_V_DIM), lambda bi, i: (bi, 0, i, 0))
    out = jax.ShapeDtypeStruct((B, DA_HEADS, S, DA_V_DIM), BF16)
    return pl.pallas_call(
        _da_proj_kernel,
        grid=(B, S // DA_PROJ_ROWS),
        in_specs=[row, _resident((D_MODEL, 3 * DA_WIDTH)), tab, tab],
        out_specs=[heads, heads, heads],
        out_shape=[out, out, out],
        compiler_params=_params(2),
        name="da_proj",
    )(x, w_qkv, cos_t, sin_t)


DA_ONES_ROWS = 16


def _da_attn_kernel(q_ref, k_ref, v_ref, lq_ref, lk_ref, sg_ref, o_ref, vt_ref, *, lam_init):
    S = q_ref.shape[0]
    tq = DA_QBLOCK
    vt_ref[:DA_V_DIM, :] = v_ref[...].astype(F32).T.astype(BF16)
    vt_ref[DA_V_DIM:, :] = jnp.ones((DA_ONES_ROWS, S), BF16)
    lqk = lq_ref[...] * lk_ref[...]
    lam = (jnp.exp(jnp.sum(lqk[0:1], axis=-1, keepdims=True))
           - jnp.exp(jnp.sum(lqk[1:2], axis=-1, keepdims=True)) + lam_init)
    lane = lax.broadcasted_iota(jnp.int32, (1, DA_V_DIM), 1)
    map0 = lane < DA_HEAD_DIM
    sub_g = sg_ref[...] * (1.0 - lam_init)
    key = lax.broadcasted_iota(jnp.int32, (tq, 2 * tq), 0)
    qry = lax.broadcasted_iota(jnp.int32, (tq, 2 * tq), 1)
    causal = key <= jnp.where(qry >= tq, qry - tq, qry)
    neg = jnp.finfo(F32).min

    def scores(i):
        n0 = i * tq
        q = q_ref[n0:n0 + tq, :]
        zero = jnp.zeros_like(q)
        q2 = jnp.concatenate([jnp.where(map0, q, zero), jnp.where(map0, zero, q)], axis=0)
        s_d = jnp.where(causal, _dot_nt(k_ref[n0:n0 + tq, :], q2), neg)
        s_p = _dot_nt(k_ref[:n0, :], q2) if i > 0 else None
        return s_p, s_d

    def finish(i, s_p, s_d):
        n0 = i * tq
        m = jnp.max(s_d, axis=0, keepdims=True)
        if i > 0:
            m = jnp.maximum(m, jnp.max(s_p, axis=0, keepdims=True))
            acc = _dot(vt_ref[:, :n0], jnp.exp2(s_p - m).astype(BF16))
            acc = acc + _dot(vt_ref[:, n0:n0 + tq], jnp.exp2(s_d - m).astype(BF16))
        else:
            acc = _dot(vt_ref[:, n0:n0 + tq], jnp.exp2(s_d - m).astype(BF16))
        r = acc[:DA_V_DIM] / acc[DA_V_DIM:DA_V_DIM + 1]
        o = r[:, :tq] - lam * r[:, tq:]
        o = o * lax.rsqrt(jnp.mean(o * o, axis=0, keepdims=True) + LN_EPS) * sub_g
        o_ref[0, n0:n0 + tq, :] = o.T.astype(BF16)

    nb = S // tq
    pending = scores(0)
    for i in range(nb):
        upcoming = scores(i + 1) if i + 1 < nb else None
        finish(i, *pending)
        pending = upcoming


def _da_attn(q, k, v, lam_q, lam_k, subln_g, j, lam_init):
    B, _, S, _ = q.shape
    head = pl.BlockSpec((None, None, S, DA_V_DIM), lambda bi, h: (bi, h, 0, 0))
    return pl.pallas_call(
        functools.partial(_da_attn_kernel, lam_init=lam_init),
        grid=(B, DA_HEADS),
        in_specs=[head, head, head, _resident((2, DA_HEAD_DIM), (j,)),
                  _resident((2, DA_HEAD_DIM), (j,)), _resident((DA_V_DIM, 1), (j,))],
        out_specs=pl.BlockSpec((1, S, DA_V_DIM), lambda bi, h: (bi, 0, h)),
        out_shape=jax.ShapeDtypeStruct((B, S, DA_WIDTH), BF16),
        scratch_shapes=[pltpu.VMEM((DA_V_DIM + DA_ONES_ROWS, S), BF16)],
        compiler_params=_params(2),
        name="da_attn",
    )(q, k, v, lam_q, lam_k, subln_g)


POOL_ROWS = 512
POOL_SKIP = 8
assert POOL_WINDOWS == tuple(2 ** (l + 1) for l in range(len(POOL_WINDOWS)))
assert POOL_HALO >= POOL_SKIP + sum(w // 2 for w in POOL_WINDOWS) and POOL_HALO % 8 == 0


def _pool_ln_kernel(x_ref, prev_ref, w_ref, bias_ref, scale_ref, g_ref, b_ref, o_ref, *bufs,
                    tiles_per_seq):
    i = pl.program_id(0)
    t0 = (i % tiles_per_seq) * POOL_ROWS
    x = x_ref[...]
    first = (i % tiles_per_seq) == 0
    n_rows = POOL_HALO + POOL_ROWS
    bufs[0][0:POOL_HALO, :] = jnp.where(first, 0.0, prev_ref[...])
    bufs[0][POOL_HALO:, :] = x
    t = t0 + lax.broadcasted_iota(jnp.int32, (POOL_ROWS, 1), 0)
    outs = []
    for level, w in enumerate(POOL_WINDOWS):
        src = bufs[level]
        shift = w // 2
        tot = src[POOL_SKIP:n_rows, :] + src[POOL_SKIP - shift:n_rows - shift, :]
        if level + 1 < len(POOL_WINDOWS):
            bufs[level + 1][0:POOL_SKIP, :] = jnp.zeros((POOL_SKIP, tot.shape[1] - POOL_GROUP_DIM), F32)
            bufs[level + 1][POOL_SKIP:, :] = tot[:, POOL_GROUP_DIM:]
        xg = x[:, level * POOL_GROUP_DIM:(level + 1) * POOL_GROUP_DIM]
        cnt = jnp.minimum(t + 1, w).astype(F32)
        pooled = tot[POOL_HALO - POOL_SKIP:, 0:POOL_GROUP_DIM] / cnt - xg
        outs.append(_dot(pooled.astype(BF16), w_ref[level]))
    y = (jnp.concatenate(outs, axis=-1) + bias_ref[...]) * scale_ref[...]
    o_ref[...] = _layer_norm(ALPHA * x + y, g_ref[...], b_ref[...])


def _pool_ln(x2d, seq_len, w, bias_all, scale_all, j, ln_g, ln_b, layer):
    T = x2d.shape[0]
    tiles_per_seq = seq_len // POOL_ROWS
    halo_blocks = POOL_ROWS // POOL_HALO
    return pl.pallas_call(
        functools.partial(_pool_ln_kernel, tiles_per_seq=tiles_per_seq),
        grid=(T // POOL_ROWS,),
        in_specs=[pl.BlockSpec((POOL_ROWS, D_MODEL), lambda i: (i, 0)),
                  pl.BlockSpec((POOL_HALO, D_MODEL),
                               lambda i: (jnp.maximum(i * halo_blocks - 1, 0), 0)),
                  _resident(w.shape),
                  _resident((1, D_MODEL), (j,)), _resident((1, D_MODEL), (j,)),
                  _resident((1, D_MODEL), (layer, 1)), _resident((1, D_MODEL), (layer, 1))],
        out_specs=pl.BlockSpec((POOL_ROWS, D_MODEL), lambda i: (i, 0)),
        out_shape=jax.ShapeDtypeStruct((T, D_MODEL), F32),
        scratch_shapes=[pltpu.VMEM((POOL_HALO + POOL_ROWS, D_MODEL - l * POOL_GROUP_DIM), F32)
                        for l in range(len(POOL_WINDOWS))],
        compiler_params=_params(1),
        name="pool_ln",
    )(x2d, x2d, w, bias_all, scale_all, ln_g, ln_b)


RET_PROJ_ROWS = 512
RET_QK_WIDTH = RET_HEADS * RET_QK_DIM
RET_V_WIDTH = RET_HEADS * RET_V_DIM
RET_STEP_ROWS = 512


def _ret_proj_kernel(x_ref, w_ref, cos_ref, sin_ref, q_ref, k_ref, v_ref, sg_ref):
    xb = x_ref[0].astype(BF16)
    cos = cos_ref[0]
    sin = sin_ref[0]
    half = RET_QK_DIM // 2
    for part, out_ref, scale in ((0, q_ref, 1.0), (1, k_ref, RET_QK_DIM ** -0.5)):
        y = _dot(xb, w_ref[:, part * RET_QK_WIDTH:(part + 1) * RET_QK_WIDTH])
        for h in range(RET_HEADS):
            x1 = y[:, h * RET_QK_DIM:h * RET_QK_DIM + half]
            x2 = y[:, h * RET_QK_DIM + half:(h + 1) * RET_QK_DIM]
            out_ref[0, :, h * RET_QK_DIM:h * RET_QK_DIM + half] = (
                (x1 * cos - x2 * sin) * scale).astype(BF16)
            out_ref[0, :, h * RET_QK_DIM + half:(h + 1) * RET_QK_DIM] = (
                (x2 * cos + x1 * sin) * scale).astype(BF16)
    v0 = 2 * RET_QK_WIDTH
    g0 = v0 + RET_V_WIDTH
    for c in range(0, RET_V_WIDTH, 1024):
        v_ref[0, :, c:c + 1024] = _dot(xb, w_ref[:, v0 + c:v0 + c + 1024]).astype(BF16)
        gate = _dot(xb, w_ref[:, g0 + c:g0 + c + 1024])
        sg_ref[0, :, c:c + 1024] = (gate * jax.nn.sigmoid(gate)).astype(BF16)


def _ret_proj(x, w, cos_t, sin_t):
    B, S, _ = x.shape
    grid_row = lambda width: pl.BlockSpec((1, RET_PROJ_ROWS, width), lambda bi, i: (bi, i, 0))
    qk = jax.ShapeDtypeStruct((B, S, RET_QK_WIDTH), BF16)
    vg = jax.ShapeDtypeStruct((B, S, RET_V_WIDTH), BF16)
    return pl.pallas_call(
        _ret_proj_kernel,
        grid=(B, S // RET_PROJ_ROWS),
        in_specs=[grid_row(D_MODEL), _resident(w.shape),
                  grid_row(LANES), grid_row(LANES)],
        out_specs=[grid_row(RET_QK_WIDTH), grid_row(RET_QK_WIDTH),
                   grid_row(RET_V_WIDTH), grid_row(RET_V_WIDTH)],
        out_shape=[qk, qk, vg, vg],
        compiler_params=_params(2),
        name="ret_proj",
    )(x, w, cos_t, sin_t)


def _ret_core_kernel(q_ref, k_ref, v_ref, sg_ref, o_ref, state_ref):
    C = RET_CHUNK

    @pl.when(pl.program_id(1) == 0)
    def _():
        state_ref[...] = jnp.zeros_like(state_ref)

    ri = lax.broadcasted_iota(jnp.int32, (C, C), 0)
    ci = lax.broadcasted_iota(jnp.int32, (C, C), 1)
    rel = (ri - ci).astype(F32)
    idx = lax.broadcasted_iota(jnp.int32, (C, 1), 0).astype(F32)
    decays = []
    for h in range(RET_HEADS):
        log_gamma = math.log(1.0 - 2.0 ** (-5.0 - h))
        decays.append((jnp.where(rel >= 0, jnp.exp(jnp.maximum(rel, 0.0) * log_gamma), 0.0),
                       jnp.exp((idx + 1.0) * log_gamma),
                       jnp.exp((C - 1.0 - idx) * log_gamma),
                       math.exp(C * log_gamma)))
    for c in range(RET_STEP_ROWS // C):
        rows = slice(c * C, (c + 1) * C)
        for h in range(RET_HEADS):
            d_intra, q_decay, k_decay, chunk_decay = decays[h]
            qs = slice(h * RET_QK_DIM, (h + 1) * RET_QK_DIM)
            vs = slice(h * RET_V_DIM, (h + 1) * RET_V_DIM)
            qc = q_ref[0, rows, qs]
            kc = k_ref[0, rows, qs]
            vc = v_ref[0, rows, vs]
            state = state_ref[h]
            att = _dot_nt(qc, kc) * d_intra
            inner = _dot(att.astype(BF16), vc)
            cross = _dot(qc, state.astype(BF16)) * q_decay
            kd = (kc.astype(F32) * k_decay).astype(BF16)
            state_ref[h] = state * chunk_decay + _dot_tn(kd, vc)
            o = inner + cross
            mu = jnp.mean(o, axis=-1, keepdims=True)
            d = o - mu
            var = jnp.mean(d * d, axis=-1, keepdims=True)
            on = d * lax.rsqrt(var + LN_EPS)
            o_ref[0, rows, vs] = (sg_ref[0, rows, vs].astype(F32) * on).astype(BF16)


def _ret_core(q, k, v, sg):
    B, S, _ = q.shape
    blk = lambda width: pl.BlockSpec((1, RET_STEP_ROWS, width), lambda bi, i: (bi, i, 0))
    return pl.pallas_call(
        _ret_core_kernel,
        grid=(B, S // RET_STEP_ROWS),
        in_specs=[blk(RET_QK_WIDTH), blk(RET_QK_WIDTH), blk(RET_V_WIDTH), blk(RET_V_WIDTH)],
        out_specs=blk(RET_V_WIDTH),
        out_shape=jax.ShapeDtypeStruct((B, S, RET_V_WIDTH), BF16),
        scratch_shapes=[pltpu.VMEM((RET_HEADS, RET_QK_DIM, RET_V_DIM), F32)],
        compiler_params=_params(2),
        name="ret_core",
    )(q, k, v, sg)


def _da_rope_tables(pos):
    inv = 1.0 / (ROPE_THETA ** (jnp.arange(0, DA_HEAD_DIM, 2, dtype=F32) / DA_HEAD_DIM))
    ang = pos * inv[None, None, :]
    cos, sin = jnp.cos(ang), jnp.sin(ang)
    return (jnp.concatenate([cos, cos, cos, cos], axis=-1),
            jnp.concatenate([-sin, sin, -sin, sin], axis=-1))


def _ret_rope_tables(pos):
    inv = 1.0 / (ROPE_THETA ** jnp.linspace(0.0, 1.0, RET_QK_DIM // 2, dtype=F32))
    ang = pos * inv[None, None, :]
    return jnp.cos(ang), jnp.sin(ang)


def kernel(x, mem, positions, ffn_w_in, ffn_w_out, ln_g, ln_b, da_w_qkv, da_w_o, da_lam_q, da_lam_k, da_subln_g, pool_w, pool_b, pool_scale, ret_w_qkvg, ret_w_o, xa_wq, xa_wkv, xa_wo):
    B, S, D = x.shape
    T = B * S
    pos = positions.astype(F32)[..., None]
    da_cos, da_sin = _da_rope_tables(pos)
    ret_cos, ret_sin = _ret_rope_tables(pos)
    kv_all = _mem_kv(mem.reshape(-1, D), xa_wkv).reshape(DEPTH, B, mem.shape[1], 2 * D)
    ln_g = ln_g.reshape(DEPTH, 4, 1, D)
    ln_b = ln_b.reshape(DEPTH, 4, 1, D)
    pool_w = pool_w.reshape(-1, D, POOL_GROUP_DIM)
    pool_b = pool_b.reshape(-1, 1, D)
    pool_scale = pool_scale.reshape(-1, 1, D)
    da_subln_g = da_subln_g.reshape(-1, DA_V_DIM, 1)

    def layer_weights(i):
        m, j = i % N_MIXERS, i // N_MIXERS
        mixer = ([("qkv", da_w_qkv, (j,)), ("o", da_w_o, (j,))] if m == 0 else
                 [("pool", pool_w, (j,))] if m == 1 else
                 [("qkvg", ret_w_qkvg, (j,)), ("o", ret_w_o, (j,))])
        return mixer + [("xa_wq", xa_wq, (i,)), ("xa_wo", xa_wo, (i,)),
                        ("w_in", ffn_w_in, (i, 1)), ("w_out", ffn_w_out, (i, 1))]

    def ffn(x, w, i, ln_idx, items):
        y, cast = _ffn_ln(x, w["w_in"], w["w_out"], ln_g, ln_b, i, ln_idx,
                          [(arr, lead) for _, arr, lead in items])
        return y, {name: c for (name, _, _), c in zip(items, cast)}

    w = {"w_in": ffn_w_in[0, 0].astype(BF16), "w_out": ffn_w_out[0, 0].astype(BF16)}
    x = x.reshape(T, D)
    for i in range(DEPTH):
        m, j = i % N_MIXERS, i // N_MIXERS
        x, w = ffn(x, w, i, 0, layer_weights(i))
        mixer = None
        if m == 0:
            lam_init = 0.8 - 0.6 * math.exp(-0.3 * i)
            q, k, v = _da_proj(x.reshape(B, S, D), w["qkv"], da_cos, da_sin)
            mixer = (_da_attn(q, k, v, da_lam_q, da_lam_k, da_subln_g, j, lam_init), w["o"])
        elif m == 1:
            pool_w_bf16 = w["pool"].reshape(len(POOL_WINDOWS), POOL_GROUP_DIM, POOL_GROUP_DIM)
            x = _pool_ln(x, S, pool_w_bf16, pool_b, pool_scale, j, ln_g, ln_b, i)
        else:
            q, k, v, sg = _ret_proj(x.reshape(B, S, D), w["qkvg"], ret_cos, ret_sin)
            mixer = (_ret_core(q, k, v, sg), w["o"])
        x = _xattn_ln(x.reshape(B, S, D), kv_all, w["xa_wq"], w["xa_wo"], ln_g, ln_b, i,
                      mixer).reshape(T, D)
        nxt = ([("w_in", ffn_w_in, (i + 1, 0)), ("w_out", ffn_w_out, (i + 1, 0))]
               if i + 1 < DEPTH else [])
        x, w = ffn(x, w, i, 3, nxt)
    return x.reshape(B, S, D)
```

```python
import functools
import math

import jax
import jax.numpy as jnp
from jax import lax
from jax.experimental import pallas as pl
from jax.experimental.pallas import tpu as pltpu

D_MODEL = 1024
DEPTH = 4
N_MIXERS = 3
DA_HEADS = 8
DA_HEAD_DIM = 64
DA_V_DIM = 2 * DA_HEAD_DIM
ROPE_THETA = 10000.0
POOL_WINDOWS = (2, 4, 8, 16)
POOL_GROUP_DIM = D_MODEL // len(POOL_WINDOWS)
POOL_HALO = 32
RET_HEADS = 4
RET_QK_DIM = D_MODEL // RET_HEADS
RET_V_DIM = 2 * RET_QK_DIM
RET_CHUNK = 128
XA_HEADS = 4
XA_HEAD_DIM = D_MODEL // XA_HEADS
D_FF = 2816
ALPHA = (2 * DEPTH) ** 0.25
LN_EPS = 1e-5
LOG2E = math.log2(math.e)

BF16 = jnp.bfloat16
F32 = jnp.float32

V7X_VMEM_LIMIT_BYTES = 56 * 1024 * 1024
LANES = 128


def _params(n_axes):
    return pltpu.CompilerParams(
        dimension_semantics=("arbitrary",) * n_axes,
        vmem_limit_bytes=V7X_VMEM_LIMIT_BYTES)


def _resident(tail, lead=()):
    shape = (None,) * len(lead) + tuple(tail)
    index = tuple(lead) + (0,) * len(tail)
    return pl.BlockSpec(shape, lambda *_: index, pipeline_mode=pl.Buffered(1))


def _layer_norm(y, g, b):
    mu = jnp.mean(y, axis=-1, keepdims=True)
    d = y - mu
    var = jnp.mean(d * d, axis=-1, keepdims=True)
    return d * lax.rsqrt(var + LN_EPS) * g + b


def _dot(a, b):
    return jnp.dot(a, b, preferred_element_type=F32)


def _dot_nt(a, b):
    return lax.dot_general(a, b, (((1,), (1,)), ((), ())), preferred_element_type=F32)


def _dot_tn(a, b):
    return lax.dot_general(a, b, (((0,), (0,)), ((), ())), preferred_element_type=F32)


def _matmul_kernel(a_ref, w_ref, o_ref):
    o_ref[...] = _dot(a_ref[...].astype(BF16), w_ref[...].astype(BF16)).astype(o_ref.dtype)


def _mem_kv(mem2d, wkv):
    R = mem2d.shape[0]
    L, _, N = wkv.shape
    return pl.pallas_call(
        _matmul_kernel,
        grid=(L,),
        in_specs=[_resident((R, D_MODEL)),
                  pl.BlockSpec((None, D_MODEL, N), lambda l: (l, 0, 0))],
        out_specs=pl.BlockSpec((None, R, N), lambda l: (l, 0, 0)),
        out_shape=jax.ShapeDtypeStruct((L, R, N), BF16),
        compiler_params=_params(1),
        name="mem_kv",
    )(mem2d, wkv)


FFN_ROWS = 1024
FFN_SUB = 512
FFN_CHUNK = 512


BF16_SUBLANES = 16


def _cast_slab_rows(rows, steps):
    return min(r for r in range(BF16_SUBLANES, rows + 1, BF16_SUBLANES)
               if rows % r == 0 and rows // r <= steps)


def _ffn_ln_kernel(*refs, n_cast):
    x_ref, win_ref, wout_ref, g_ref, b_ref = refs[:5]
    o_ref = refs[5 + n_cast]
    for src_ref, dst_ref in zip(refs[5:5 + n_cast], refs[6 + n_cast:]):
        dst_ref[...] = src_ref[...].astype(BF16)
    for r0 in range(0, FFN_ROWS, FFN_SUB):
        x = x_ref[r0:r0 + FFN_SUB, :]
        xb = x.astype(BF16)
        acc = jnp.zeros(x.shape, F32)
        for c0 in range(0, D_FF, FFN_CHUNK):
            c1 = min(c0 + FFN_CHUNK, D_FF)
            hg = _dot(xb, win_ref[:, c0:c1])
            hu = _dot(xb, win_ref[:, D_FF + c0:D_FF + c1])
            act = (hg * jax.nn.sigmoid(hg) * hu).astype(BF16)
            acc = acc + _dot(act, wout_ref[c0:c1, :])
        o_ref[r0:r0 + FFN_SUB, :] = _layer_norm(ALPHA * x + 0.5 * acc, g_ref[...], b_ref[...])


def _ffn_ln(x2d, w_in, w_out, ln_g, ln_b, layer, ln_idx, to_cast):
    T = x2d.shape[0]
    steps = T // FFN_ROWS
    row = pl.BlockSpec((FFN_ROWS, D_MODEL), lambda i: (i, 0))
    cast_specs_in, cast_specs_out, cast_shapes = [], [], []
    for arr, lead in to_cast:
        rows, cols = arr.shape[-2:]
        slab = _cast_slab_rows(rows, steps)
        last = rows // slab - 1
        lead = tuple(lead)
        cast_specs_in.append(pl.BlockSpec(
            (None,) * len(lead) + (slab, cols),
            lambda i, lead=lead, last=last: lead + (jnp.minimum(i, last), 0)))
        cast_specs_out.append(pl.BlockSpec((slab, cols),
                                           lambda i, last=last: (jnp.minimum(i, last), 0)))
        cast_shapes.append(jax.ShapeDtypeStruct((rows, cols), BF16))
    outs = pl.pallas_call(
        functools.partial(_ffn_ln_kernel, n_cast=len(to_cast)),
        grid=(steps,),
        in_specs=[row,
                  _resident((D_MODEL, 2 * D_FF)),
                  _resident((D_FF, D_MODEL)),
                  _resident((1, D_MODEL), (layer, ln_idx)),
                  _resident((1, D_MODEL), (layer, ln_idx))] + cast_specs_in,
        out_specs=[row] + cast_specs_out,
        out_shape=[jax.ShapeDtypeStruct((T, D_MODEL), F32)] + cast_shapes,
        compiler_params=_params(1),
        name="ffn_ln",
    )(x2d, w_in, w_out, ln_g, ln_b, *[arr for arr, _ in to_cast])
    return outs[0], list(outs[1:])


XA_ROWS = 1024
XA_SUB = 256


def _xattn_ln_kernel(*refs, mixer_proj):
    x_ref = refs[0]
    kv_ref, wq_ref, wo_ref, g_ref, b_ref, o_ref = refs[-6:]

    def project(r0):
        x = x_ref[0, r0:r0 + XA_SUB, :]
        if mixer_proj:
            a_ref, wm_ref, gm_ref, bm_ref = refs[1:5]
            x = _layer_norm(ALPHA * x + _dot(a_ref[0, r0:r0 + XA_SUB, :], wm_ref[...]),
                            gm_ref[...], bm_ref[...])
        return x, (_dot(x.astype(BF16), wq_ref[...]) * (XA_HEAD_DIM ** -0.5 * LOG2E)).astype(BF16)

    def finish(r0, x, h_out):
        o_ref[0, r0:r0 + XA_SUB, :] = _layer_norm(ALPHA * x + h_out, g_ref[...], b_ref[...])

    starts = list(range(0, XA_ROWS, XA_SUB))
    x, q = project(starts[0])
    lagged = None
    for n, r0 in enumerate(starts):
        s = [_dot_nt(q[:, h * XA_HEAD_DIM:(h + 1) * XA_HEAD_DIM],
                     kv_ref[:, h * XA_HEAD_DIM:(h + 1) * XA_HEAD_DIM]) for h in range(XA_HEADS)]
        upcoming = project(starts[n + 1]) if n + 1 < len(starts) else None
        heads = []
        for h in range(XA_HEADS):
            e = jnp.exp2(s[h] - jnp.max(s[h], axis=-1, keepdims=True))
            p = e * (1.0 / jnp.sum(e, axis=-1, keepdims=True))
            lo = D_MODEL + h * XA_HEAD_DIM
            heads.append(_dot(p.astype(BF16), kv_ref[:, lo:lo + XA_HEAD_DIM]).astype(BF16))
        if lagged is not None:
            finish(*lagged)
        lagged = (r0, x, _dot(jnp.concatenate(heads, axis=-1), wo_ref[...]))
        if upcoming is not None:
            x, q = upcoming
    finish(*lagged)


def _xattn_ln(x, kv_all, wq, wo, ln_g, ln_b, layer, mixer=None):
    B, S, _ = x.shape
    M = kv_all.shape[2]
    row = pl.BlockSpec((1, XA_ROWS, D_MODEL), lambda bi, i: (bi, i, 0))
    mixer_specs, mixer_args = [], []
    if mixer is not None:
        a, w_mix = mixer
        K = a.shape[-1]
        mixer_specs = [pl.BlockSpec((1, XA_ROWS, K), lambda bi, i: (bi, i, 0)),
                       _resident((K, D_MODEL)),
                       _resident((1, D_MODEL), (layer, 1)), _resident((1, D_MODEL), (layer, 1))]
        mixer_args = [a, w_mix, ln_g, ln_b]
    return pl.pallas_call(
        functools.partial(_xattn_ln_kernel, mixer_proj=mixer is not None),
        grid=(B, S // XA_ROWS),
        in_specs=[row] + mixer_specs + [
            pl.BlockSpec((None, None, M, 2 * D_MODEL), lambda bi, i: (layer, bi, 0, 0)),
            _resident((D_MODEL, D_MODEL)), _resident((D_MODEL, D_MODEL)),
            _resident((1, D_MODEL), (layer, 2)), _resident((1, D_MODEL), (layer, 2))],
        out_specs=row,
        out_shape=jax.ShapeDtypeStruct((B, S, D_MODEL), F32),
        compiler_params=_params(2),
        name="xattn_ln",
    )(x, *mixer_args, kv_all, wq, wo, ln_g, ln_b)


DA_PROJ_ROWS = 512
DA_QBLOCK = 512
DA_WIDTH = DA_HEADS * DA_V_DIM


def _da_proj_kernel(x_ref, w_ref, cos_ref, sin_ref, q_ref, k_ref, v_ref):
    xb = x_ref[0].astype(BF16)
    cos = cos_ref[0]
    sin = sin_ref[0]
    lane = lax.broadcasted_iota(jnp.int32, (1, DA_V_DIM), 1)
    first_half = (lane % DA_HEAD_DIM) < (DA_HEAD_DIM // 2)
    q_scale = DA_HEAD_DIM ** -0.5 * LOG2E
    for part, out_ref, scale in ((0, q_ref, q_scale), (1, k_ref, 1.0)):
        y = _dot(xb, w_ref[:, part * DA_WIDTH:(part + 1) * DA_WIDTH])
        for h in range(DA_HEADS):
            yh = y[:, h * DA_V_DIM:(h + 1) * DA_V_DIM]
            partner = jnp.where(first_half,
                                pltpu.roll(yh, DA_V_DIM - DA_HEAD_DIM // 2, axis=1),
                                pltpu.roll(yh, DA_HEAD_DIM // 2, axis=1))
            r = yh * cos + partner * sin
            out_ref[h] = (r * scale).astype(BF16)
    v = _dot(xb, w_ref[:, 2 * DA_WIDTH:]).astype(BF16)
    for h in range(DA_HEADS):
        v_ref[h] = v[:, h * DA_V_DIM:(h + 1) * DA_V_DIM]


def _da_proj(x, w_qkv, cos_t, sin_t):
    B, S, _ = x.shape
    row = pl.BlockSpec((1, DA_PROJ_ROWS, D_MODEL), lambda bi, i: (bi, i, 0))
    tab = pl.BlockSpec((1, DA_PROJ_ROWS, LANES), lambda bi, i: (bi, i, 0))
    heads = pl.BlockSpec((None, DA_HEADS, DA_PROJ_ROWS, DA_V_DIM), lambda bi, i: (bi, 0, i, 0))
    out = jax.ShapeDtypeStruct((B, DA_HEADS, S, DA_V_DIM), BF16)
    return pl.pallas_call(
        _da_proj_kernel,
        grid=(B, S // DA_PROJ_ROWS),
        in_specs=[row, _resident((D_MODEL, 3 * DA_WIDTH)), tab, tab],
        out_specs=[heads, heads, heads],
        out_shape=[out, out, out],
        compiler_params=_params(2),
        name="da_proj",
    )(x, w_qkv, cos_t, sin_t)


DA_ONES_ROWS = 16


DA_HEADS_PER_STEP = 4


def _da_attn_kernel(q_ref, k_ref, v_ref, lq_ref, lk_ref, sg_ref, o_ref, vt_ref, *, lam_init):
    G = DA_HEADS_PER_STEP
    S = q_ref.shape[1]
    tq = DA_QBLOCK
    for g in range(G):
        vt_ref[g, :DA_V_DIM, :] = v_ref[g].astype(F32).T.astype(BF16)
        vt_ref[g, DA_V_DIM:, :] = jnp.ones((DA_ONES_ROWS, S), BF16)
    lqk = lq_ref[...] * lk_ref[...]
    lam = (jnp.exp(jnp.sum(lqk[0:1], axis=-1, keepdims=True))
           - jnp.exp(jnp.sum(lqk[1:2], axis=-1, keepdims=True)) + lam_init)
    lane = lax.broadcasted_iota(jnp.int32, (1, DA_V_DIM), 1)
    map0 = lane < DA_HEAD_DIM
    sub_g = sg_ref[...] * (1.0 - lam_init)
    key = lax.broadcasted_iota(jnp.int32, (tq, 2 * tq), 0)
    qry = lax.broadcasted_iota(jnp.int32, (tq, 2 * tq), 1)
    causal = key <= jnp.where(qry >= tq, qry - tq, qry)
    neg = jnp.finfo(F32).min

    def scores(g, i):
        n0 = i * tq
        q = q_ref[g, n0:n0 + tq, :]
        zero = jnp.zeros_like(q)
        q2 = jnp.concatenate([jnp.where(map0, q, zero), jnp.where(map0, zero, q)], axis=0)
        s_d = jnp.where(causal, _dot_nt(k_ref[g, n0:n0 + tq, :], q2), neg)
        s_p = _dot_nt(k_ref[g, :n0, :], q2) if i > 0 else None
        return s_p, s_d

    def finish(g, i, s_p, s_d):
        n0 = i * tq
        m = jnp.max(s_d, axis=0, keepdims=True)
        if i > 0:
            m = jnp.maximum(m, jnp.max(s_p, axis=0, keepdims=True))
            acc = _dot(vt_ref[g, :, :n0], jnp.exp2(s_p - m).astype(BF16))
            acc = acc + _dot(vt_ref[g, :, n0:n0 + tq], jnp.exp2(s_d - m).astype(BF16))
        else:
            acc = _dot(vt_ref[g, :, n0:n0 + tq], jnp.exp2(s_d - m).astype(BF16))
        r = acc[:DA_V_DIM] / acc[DA_V_DIM:DA_V_DIM + 1]
        o = r[:, :tq] - lam * r[:, tq:]
        o = o * lax.rsqrt(jnp.mean(o * o, axis=0, keepdims=True) + LN_EPS) * sub_g
        o_ref[0, n0:n0 + tq, g * DA_V_DIM:(g + 1) * DA_V_DIM] = o.T.astype(BF16)

    items = [(g, i) for i in range(S // tq) for g in range(G)]
    pending = scores(*items[0])
    for n, (g, i) in enumerate(items):
        upcoming = scores(*items[n + 1]) if n + 1 < len(items) else None
        finish(g, i, *pending)
        pending = upcoming


def _da_attn(q, k, v, lam_q, lam_k, subln_g, j, lam_init):
    B, _, S, _ = q.shape
    G = DA_HEADS_PER_STEP
    head = pl.BlockSpec((None, G, S, DA_V_DIM), lambda bi, h: (bi, h, 0, 0))
    return pl.pallas_call(
        functools.partial(_da_attn_kernel, lam_init=lam_init),
        grid=(B, DA_HEADS // G),
        in_specs=[head, head, head, _resident((2, DA_HEAD_DIM), (j,)),
                  _resident((2, DA_HEAD_DIM), (j,)), _resident((DA_V_DIM, 1), (j,))],
        out_specs=pl.BlockSpec((1, S, G * DA_V_DIM), lambda bi, h: (bi, 0, h)),
        out_shape=jax.ShapeDtypeStruct((B, S, DA_WIDTH), BF16),
        scratch_shapes=[pltpu.VMEM((G, DA_V_DIM + DA_ONES_ROWS, S), BF16)],
        compiler_params=_params(2),
        name="da_attn",
    )(q, k, v, lam_q, lam_k, subln_g)


POOL_ROWS = 512
POOL_SKIP = 8
assert POOL_WINDOWS == tuple(2 ** (l + 1) for l in range(len(POOL_WINDOWS)))
assert POOL_HALO >= POOL_SKIP + sum(w // 2 for w in POOL_WINDOWS) and POOL_HALO % 8 == 0


def _pool_ln_kernel(x_ref, prev_ref, w_ref, bias_ref, scale_ref, g_ref, b_ref, o_ref, *bufs,
                    tiles_per_seq):
    i = pl.program_id(0)
    t0 = (i % tiles_per_seq) * POOL_ROWS
    x = x_ref[...]
    first = (i % tiles_per_seq) == 0
    n_rows = POOL_HALO + POOL_ROWS
    bufs[0][0:POOL_HALO, :] = jnp.where(first, 0.0, prev_ref[...])
    bufs[0][POOL_HALO:, :] = x
    t = t0 + lax.broadcasted_iota(jnp.int32, (POOL_ROWS, 1), 0)
    outs = []
    for level, w in enumerate(POOL_WINDOWS):
        src = bufs[level]
        shift = w // 2
        tot = src[POOL_SKIP:n_rows, :] + src[POOL_SKIP - shift:n_rows - shift, :]
        if level + 1 < len(POOL_WINDOWS):
            bufs[level + 1][0:POOL_SKIP, :] = jnp.zeros((POOL_SKIP, tot.shape[1] - POOL_GROUP_DIM), F32)
            bufs[level + 1][POOL_SKIP:, :] = tot[:, POOL_GROUP_DIM:]
        xg = x[:, level * POOL_GROUP_DIM:(level + 1) * POOL_GROUP_DIM]
        cnt = jnp.minimum(t + 1, w).astype(F32)
        pooled = tot[POOL_HALO - POOL_SKIP:, 0:POOL_GROUP_DIM] / cnt - xg
        outs.append(_dot(pooled.astype(BF16), w_ref[level]))
    y = (jnp.concatenate(outs, axis=-1) + bias_ref[...]) * scale_ref[...]
    o_ref[...] = _layer_norm(ALPHA * x + y, g_ref[...], b_ref[...])


def _pool_ln(x2d, seq_len, w, bias_all, scale_all, j, ln_g, ln_b, layer):
    T = x2d.shape[0]
    tiles_per_seq = seq_len // POOL_ROWS
    halo_blocks = POOL_ROWS // POOL_HALO
    return pl.pallas_call(
        functools.partial(_pool_ln_kernel, tiles_per_seq=tiles_per_seq),
        grid=(T // POOL_ROWS,),
        in_specs=[pl.BlockSpec((POOL_ROWS, D_MODEL), lambda i: (i, 0)),
                  pl.BlockSpec((POOL_HALO, D_MODEL),
                               lambda i: (jnp.maximum(i * halo_blocks - 1, 0), 0)),
                  _resident(w.shape),
                  _resident((1, D_MODEL), (j,)), _resident((1, D_MODEL), (j,)),
                  _resident((1, D_MODEL), (layer, 1)), _resident((1, D_MODEL), (layer, 1))],
        out_specs=pl.BlockSpec((POOL_ROWS, D_MODEL), lambda i: (i, 0)),
        out_shape=jax.ShapeDtypeStruct((T, D_MODEL), F32),
        scratch_shapes=[pltpu.VMEM((POOL_HALO + POOL_ROWS, D_MODEL - l * POOL_GROUP_DIM), F32)
                        for l in range(len(POOL_WINDOWS))],
        compiler_params=_params(1),
        name="pool_ln",
    )(x2d, x2d, w, bias_all, scale_all, ln_g, ln_b)


RET_PROJ_ROWS = 512
RET_QK_WIDTH = RET_HEADS * RET_QK_DIM
RET_V_WIDTH = RET_HEADS * RET_V_DIM
RET_STEP_ROWS = 512


def _ret_proj_kernel(x_ref, w_ref, cos_ref, sin_ref, q_ref, k_ref, v_ref, sg_ref):
    xb = x_ref[0].astype(BF16)
    cos = cos_ref[0]
    sin = sin_ref[0]
    half = RET_QK_DIM // 2
    for part, out_ref, scale in ((0, q_ref, 1.0), (1, k_ref, RET_QK_DIM ** -0.5)):
        y = _dot(xb, w_ref[:, part * RET_QK_WIDTH:(part + 1) * RET_QK_WIDTH])
        for h in range(RET_HEADS):
            x1 = y[:, h * RET_QK_DIM:h * RET_QK_DIM + half]
            x2 = y[:, h * RET_QK_DIM + half:(h + 1) * RET_QK_DIM]
            out_ref[0, :, h * RET_QK_DIM:h * RET_QK_DIM + half] = (
                (x1 * cos - x2 * sin) * scale).astype(BF16)
            out_ref[0, :, h * RET_QK_DIM + half:(h + 1) * RET_QK_DIM] = (
                (x2 * cos + x1 * sin) * scale).astype(BF16)
    v0 = 2 * RET_QK_WIDTH
    g0 = v0 + RET_V_WIDTH
    for c in range(0, RET_V_WIDTH, 1024):
        v_ref[0, :, c:c + 1024] = _dot(xb, w_ref[:, v0 + c:v0 + c + 1024]).astype(BF16)
        gate = _dot(xb, w_ref[:, g0 + c:g0 + c + 1024])
        sg_ref[0, :, c:c + 1024] = (gate * jax.nn.sigmoid(gate)).astype(BF16)


def _ret_proj(x, w, cos_t, sin_t):
    B, S, _ = x.shape
    grid_row = lambda width: pl.BlockSpec((1, RET_PROJ_ROWS, width), lambda bi, i: (bi, i, 0))
    qk = jax.ShapeDtypeStruct((B, S, RET_QK_WIDTH), BF16)
    vg = jax.ShapeDtypeStruct((B, S, RET_V_WIDTH), BF16)
    return pl.pallas_call(
        _ret_proj_kernel,
        grid=(B, S // RET_PROJ_ROWS),
        in_specs=[grid_row(D_MODEL), _resident(w.shape),
                  grid_row(LANES), grid_row(LANES)],
        out_specs=[grid_row(RET_QK_WIDTH), grid_row(RET_QK_WIDTH),
                   grid_row(RET_V_WIDTH), grid_row(RET_V_WIDTH)],
        out_shape=[qk, qk, vg, vg],
        compiler_params=_params(2),
        name="ret_proj",
    )(x, w, cos_t, sin_t)


def _ret_core_kernel(q_ref, k_ref, v_ref, sg_ref, o_ref, state_ref):
    C = RET_CHUNK

    @pl.when(pl.program_id(1) == 0)
    def _():
        state_ref[...] = jnp.zeros_like(state_ref)

    ri = lax.broadcasted_iota(jnp.int32, (C, C), 0)
    ci = lax.broadcasted_iota(jnp.int32, (C, C), 1)
    rel = (ri - ci).astype(F32)
    idx = lax.broadcasted_iota(jnp.int32, (C, 1), 0).astype(F32)
    decays = []
    for h in range(RET_HEADS):
        log_gamma = math.log(1.0 - 2.0 ** (-5.0 - h))
        decays.append((jnp.where(rel >= 0, jnp.exp(jnp.maximum(rel, 0.0) * log_gamma), 0.0),
                       jnp.exp((idx + 1.0) * log_gamma),
                       jnp.exp((C - 1.0 - idx) * log_gamma),
                       math.exp(C * log_gamma)))
    for c in range(RET_STEP_ROWS // C):
        rows = slice(c * C, (c + 1) * C)
        for h in range(RET_HEADS):
            d_intra, q_decay, k_decay, chunk_decay = decays[h]
            qs = slice(h * RET_QK_DIM, (h + 1) * RET_QK_DIM)
            vs = slice(h * RET_V_DIM, (h + 1) * RET_V_DIM)
            qc = q_ref[0, rows, qs]
            kc = k_ref[0, rows, qs]
            vc = v_ref[0, rows, vs]
            state = state_ref[h]
            att = _dot_nt(qc, kc) * d_intra
            inner = _dot(att.astype(BF16), vc)
            cross = _dot(qc, state.astype(BF16)) * q_decay
            kd = (kc.astype(F32) * k_decay).astype(BF16)
            state_ref[h] = state * chunk_decay + _dot_tn(kd, vc)
            o = inner + cross
            mu = jnp.mean(o, axis=-1, keepdims=True)
            d = o - mu
            var = jnp.mean(d * d, axis=-1, keepdims=True)
            on = d * lax.rsqrt(var + LN_EPS)
            o_ref[0, rows, vs] = (sg_ref[0, rows, vs].astype(F32) * on).astype(BF16)


def _ret_core(q, k, v, sg):
    B, S, _ = q.shape
    blk = lambda width: pl.BlockSpec((1, RET_STEP_ROWS, width), lambda bi, i: (bi, i, 0))
    return pl.pallas_call(
        _ret_core_kernel,
        grid=(B, S // RET_STEP_ROWS),
        in_specs=[blk(RET_QK_WIDTH), blk(RET_QK_WIDTH), blk(RET_V_WIDTH), blk(RET_V_WIDTH)],
        out_specs=blk(RET_V_WIDTH),
        out_shape=jax.ShapeDtypeStruct((B, S, RET_V_WIDTH), BF16),
        scratch_shapes=[pltpu.VMEM((RET_HEADS, RET_QK_DIM, RET_V_DIM), F32)],
        compiler_params=_params(2),
        name="ret_core",
    )(q, k, v, sg)


def _da_rope_tables(pos):
    inv = 1.0 / (ROPE_THETA ** (jnp.arange(0, DA_HEAD_DIM, 2, dtype=F32) / DA_HEAD_DIM))
    ang = pos * inv[None, None, :]
    cos, sin = jnp.cos(ang), jnp.sin(ang)
    return (jnp.concatenate([cos, cos, cos, cos], axis=-1),
            jnp.concatenate([-sin, sin, -sin, sin], axis=-1))


def _ret_rope_tables(pos):
    inv = 1.0 / (ROPE_THETA ** jnp.linspace(0.0, 1.0, RET_QK_DIM // 2, dtype=F32))
    ang = pos * inv[None, None, :]
    return jnp.cos(ang), jnp.sin(ang)


def kernel(x, mem, positions, ffn_w_in, ffn_w_out, ln_g, ln_b, da_w_qkv, da_w_o, da_lam_q, da_lam_k, da_subln_g, pool_w, pool_b, pool_scale, ret_w_qkvg, ret_w_o, xa_wq, xa_wkv, xa_wo):
    B, S, D = x.shape
    T = B * S
    pos = positions.astype(F32)[..., None]
    da_cos, da_sin = _da_rope_tables(pos)
    ret_cos, ret_sin = _ret_rope_tables(pos)
    kv_all = _mem_kv(mem.reshape(-1, D), xa_wkv).reshape(DEPTH, B, mem.shape[1], 2 * D)
    ln_g = ln_g.reshape(DEPTH, 4, 1, D)
    ln_b = ln_b.reshape(DEPTH, 4, 1, D)
    pool_w = pool_w.reshape(-1, D, POOL_GROUP_DIM)
    pool_b = pool_b.reshape(-1, 1, D)
    pool_scale = pool_scale.reshape(-1, 1, D)
    da_subln_g = da_subln_g.reshape(-1, DA_V_DIM, 1)

    def layer_weights(i):
        m, j = i % N_MIXERS, i // N_MIXERS
        mixer = ([("qkv", da_w_qkv, (j,)), ("o", da_w_o, (j,))] if m == 0 else
                 [("pool", pool_w, (j,))] if m == 1 else
                 [("qkvg", ret_w_qkvg, (j,)), ("o", ret_w_o, (j,))])
        return mixer + [("xa_wq", xa_wq, (i,)), ("xa_wo", xa_wo, (i,)),
                        ("w_in", ffn_w_in, (i, 1)), ("w_out", ffn_w_out, (i, 1))]

    def ffn(x, w, i, ln_idx, items):
        y, cast = _ffn_ln(x, w["w_in"], w["w_out"], ln_g, ln_b, i, ln_idx,
                          [(arr, lead) for _, arr, lead in items])
        return y, {name: c for (name, _, _), c in zip(items, cast)}

    w = {"w_in": ffn_w_in[0, 0].astype(BF16), "w_out": ffn_w_out[0, 0].astype(BF16)}
    x = x.reshape(T, D)
    for i in range(DEPTH):
        m, j = i % N_MIXERS, i // N_MIXERS
        x, w = ffn(x, w, i, 0, layer_weights(i))
        mixer = None
        if m == 0:
            lam_init = 0.8 - 0.6 * math.exp(-0.3 * i)
            q, k, v = _da_proj(x.reshape(B, S, D), w["qkv"], da_cos, da_sin)
            mixer = (_da_attn(q, k, v, da_lam_q, da_lam_k, da_subln_g, j, lam_init), w["o"])
        elif m == 1:
            pool_w_bf16 = w["pool"].reshape(len(POOL_WINDOWS), POOL_GROUP_DIM, POOL_GROUP_DIM)
            x = _pool_ln(x, S, pool_w_bf16, pool_b, pool_scale, j, ln_g, ln_b, i)
        else:
            q, k, v, sg = _ret_proj(x.reshape(B, S, D), w["qkvg"], ret_cos, ret_sin)
            mixer = (_ret_core(q, k, v, sg), w["o"])
        x = _xattn_ln(x.reshape(B, S, D), kv_all, w["xa_wq"], w["xa_wo"], ln_g, ln_b, i,
                      mixer).reshape(T, D)
        nxt = ([("w_in", ffn_w_in, (i + 1, 0)), ("w_out", ffn_w_out, (i + 1, 0))]
               if i + 1 < DEPTH else [])
        x, w = ffn(x, w, i, 3, nxt)
    return x.reshape(B, S, D)
```

```python
import functools
import math

import jax
import jax.numpy as jnp
from jax import lax
from jax.experimental import pallas as pl
from jax.experimental.pallas import tpu as pltpu

D_MODEL = 1024
DEPTH = 4
N_MIXERS = 3
DA_HEADS = 8
DA_HEAD_DIM = 64
DA_V_DIM = 2 * DA_HEAD_DIM
ROPE_THETA = 10000.0
POOL_WINDOWS = (2, 4, 8, 16)
POOL_GROUP_DIM = D_MODEL // len(POOL_WINDOWS)
POOL_HALO = 32
RET_HEADS = 4
RET_QK_DIM = D_MODEL // RET_HEADS
RET_V_DIM = 2 * RET_QK_DIM
RET_CHUNK = 128
XA_HEADS = 4
XA_HEAD_DIM = D_MODEL // XA_HEADS
D_FF = 2816
ALPHA = (2 * DEPTH) ** 0.25
LN_EPS = 1e-5
LOG2E = math.log2(math.e)

BF16 = jnp.bfloat16
F32 = jnp.float32

V7X_VMEM_LIMIT_BYTES = 56 * 1024 * 1024
LANES = 128


def _params(n_axes):
    return pltpu.CompilerParams(
        dimension_semantics=("arbitrary",) * n_axes,
        vmem_limit_bytes=V7X_VMEM_LIMIT_BYTES)


def _resident(tail, lead=()):
    shape = (None,) * len(lead) + tuple(tail)
    index = tuple(lead) + (0,) * len(tail)
    return pl.BlockSpec(shape, lambda *_: index, pipeline_mode=pl.Buffered(1))


def _layer_norm(y, g, b):
    mu = jnp.mean(y, axis=-1, keepdims=True)
    d = y - mu
    var = jnp.mean(d * d, axis=-1, keepdims=True)
    return d * lax.rsqrt(var + LN_EPS) * g + b


def _dot(a, b):
    return jnp.dot(a, b, preferred_element_type=F32)


def _dot_nt(a, b):
    return lax.dot_general(a, b, (((1,), (1,)), ((), ())), preferred_element_type=F32)


def _dot_tn(a, b):
    return lax.dot_general(a, b, (((0,), (0,)), ((), ())), preferred_element_type=F32)


def _matmul_kernel(a_ref, w_ref, o_ref):
    o_ref[...] = _dot(a_ref[...].astype(BF16), w_ref[...].astype(BF16)).astype(o_ref.dtype)


def _mem_kv(mem2d, wkv):
    R = mem2d.shape[0]
    L, _, N = wkv.shape
    return pl.pallas_call(
        _matmul_kernel,
        grid=(L,),
        in_specs=[_resident((R, D_MODEL)),
                  pl.BlockSpec((None, D_MODEL, N), lambda l: (l, 0, 0))],
        out_specs=pl.BlockSpec((None, R, N), lambda l: (l, 0, 0)),
        out_shape=jax.ShapeDtypeStruct((L, R, N), BF16),
        compiler_params=_params(1),
        name="mem_kv",
    )(mem2d, wkv)


FFN_ROWS = 1024
FFN_SUB = 512
FFN_CHUNK = 512


BF16_SUBLANES = 16


def _cast_slab_rows(rows, steps):
    return min(r for r in range(BF16_SUBLANES, rows + 1, BF16_SUBLANES)
               if rows % r == 0 and rows // r <= steps)


ROPE_PIECE_ROWS = 128


def _ffn_ln_kernel(*refs, n_cast, n_rope):
    x_ref, win_ref, wout_ref, g_ref, b_ref = refs[:5]
    n_in = 5 + n_cast + (2 if n_rope else 0)
    o_ref = refs[n_in]
    for src_ref, dst_ref in zip(refs[5:5 + n_cast], refs[n_in + 1:n_in + 1 + n_cast]):
        dst_ref[...] = src_ref[...].astype(BF16)
    rope_pieces = []
    if n_rope:
        pos_ref, const_ref = refs[5 + n_cast:n_in]
        tables = refs[n_in + 1 + n_cast:]
        rope_pieces = list(range(0, FFN_ROWS, ROPE_PIECE_ROWS))

    def rope_piece(p0):
        pos = pos_ref[p0:p0 + ROPE_PIECE_ROWS, :]
        for t in range(n_rope):
            ang = pos * const_ref[2 * t:2 * t + 1, :]
            tables[2 * t][p0:p0 + ROPE_PIECE_ROWS, :] = jnp.cos(ang)
            tables[2 * t + 1][p0:p0 + ROPE_PIECE_ROWS, :] = jnp.sin(ang) * const_ref[2 * t + 1:2 * t + 2, :]

    for r0 in range(0, FFN_ROWS, FFN_SUB):
        x = x_ref[r0:r0 + FFN_SUB, :]
        xb = x.astype(BF16)
        acc = jnp.zeros(x.shape, F32)
        for c0 in range(0, D_FF, FFN_CHUNK):
            c1 = min(c0 + FFN_CHUNK, D_FF)
            hg = _dot(xb, win_ref[:, c0:c1])
            hu = _dot(xb, win_ref[:, D_FF + c0:D_FF + c1])
            if rope_pieces:
                rope_piece(rope_pieces.pop(0))
            act = (hg * jax.nn.sigmoid(hg) * hu).astype(BF16)
            acc = acc + _dot(act, wout_ref[c0:c1, :])
        o_ref[r0:r0 + FFN_SUB, :] = _layer_norm(ALPHA * x + 0.5 * acc, g_ref[...], b_ref[...])
    assert not rope_pieces


def _ffn_ln(x2d, w_in, w_out, ln_g, ln_b, layer, ln_idx, to_cast, rope=None):
    T = x2d.shape[0]
    steps = T // FFN_ROWS
    row = pl.BlockSpec((FFN_ROWS, D_MODEL), lambda i: (i, 0))
    rope_in, rope_specs_in, rope_specs_out, rope_shapes = [], [], [], []
    n_rope = 0
    if rope is not None:
        pos_b, consts = rope
        n_rope = consts.shape[0] // 2
        lane_rows = pl.BlockSpec((FFN_ROWS, LANES), lambda i: (i, 0))
        rope_in = [pos_b, consts]
        rope_specs_in = [lane_rows, _resident(consts.shape)]
        rope_specs_out = [lane_rows] * (2 * n_rope)
        rope_shapes = [jax.ShapeDtypeStruct((T, LANES), F32)] * (2 * n_rope)
    cast_specs_in, cast_specs_out, cast_shapes = [], [], []
    for arr, lead in to_cast:
        rows, cols = arr.shape[-2:]
        slab = _cast_slab_rows(rows, steps)
        last = rows // slab - 1
        lead = tuple(lead)
        cast_specs_in.append(pl.BlockSpec(
            (None,) * len(lead) + (slab, cols),
            lambda i, lead=lead, last=last: lead + (jnp.minimum(i, last), 0)))
        cast_specs_out.append(pl.BlockSpec((slab, cols),
                                           lambda i, last=last: (jnp.minimum(i, last), 0)))
        cast_shapes.append(jax.ShapeDtypeStruct((rows, cols), BF16))
    outs = pl.pallas_call(
        functools.partial(_ffn_ln_kernel, n_cast=len(to_cast), n_rope=n_rope),
        grid=(steps,),
        in_specs=[row,
                  _resident((D_MODEL, 2 * D_FF)),
                  _resident((D_FF, D_MODEL)),
                  _resident((1, D_MODEL), (layer, ln_idx)),
                  _resident((1, D_MODEL), (layer, ln_idx))] + cast_specs_in + rope_specs_in,
        out_specs=[row] + cast_specs_out + rope_specs_out,
        out_shape=[jax.ShapeDtypeStruct((T, D_MODEL), F32)] + cast_shapes + rope_shapes,
        compiler_params=_params(1),
        name="ffn_ln",
    )(x2d, w_in, w_out, ln_g, ln_b, *[arr for arr, _ in to_cast], *rope_in)
    n_cast = len(to_cast)
    return outs[0], list(outs[1:1 + n_cast]), list(outs[1 + n_cast:])


XA_ROWS = 1024
XA_SUB = 256


def _xattn_ln_kernel(*refs, mixer_proj):
    x_ref = refs[0]
    kv_ref, wq_ref, wo_ref, g_ref, b_ref, o_ref = refs[-6:]

    def project(r0):
        x = x_ref[0, r0:r0 + XA_SUB, :]
        if mixer_proj:
            a_ref, wm_ref, gm_ref, bm_ref = refs[1:5]
            x = _layer_norm(ALPHA * x + _dot(a_ref[0, r0:r0 + XA_SUB, :], wm_ref[...]),
                            gm_ref[...], bm_ref[...])
        return x, (_dot(x.astype(BF16), wq_ref[...]) * (XA_HEAD_DIM ** -0.5 * LOG2E)).astype(BF16)

    def finish(r0, x, h_out):
        o_ref[0, r0:r0 + XA_SUB, :] = _layer_norm(ALPHA * x + h_out, g_ref[...], b_ref[...])

    starts = list(range(0, XA_ROWS, XA_SUB))
    x, q = project(starts[0])
    lagged = None
    for n, r0 in enumerate(starts):
        s = [_dot_nt(q[:, h * XA_HEAD_DIM:(h + 1) * XA_HEAD_DIM],
                     kv_ref[:, h * XA_HEAD_DIM:(h + 1) * XA_HEAD_DIM]) for h in range(XA_HEADS)]
        upcoming = project(starts[n + 1]) if n + 1 < len(starts) else None
        heads = []
        for h in range(XA_HEADS):
            e = jnp.exp2(s[h] - jnp.max(s[h], axis=-1, keepdims=True))
            p = e * (1.0 / jnp.sum(e, axis=-1, keepdims=True))
            lo = D_MODEL + h * XA_HEAD_DIM
            heads.append(_dot(p.astype(BF16), kv_ref[:, lo:lo + XA_HEAD_DIM]).astype(BF16))
        if lagged is not None:
            finish(*lagged)
        lagged = (r0, x, _dot(jnp.concatenate(heads, axis=-1), wo_ref[...]))
        if upcoming is not None:
            x, q = upcoming
    finish(*lagged)


def _xattn_ln(x, kv_all, wq, wo, ln_g, ln_b, layer, mixer=None):
    B, S, _ = x.shape
    M = kv_all.shape[2]
    row = pl.BlockSpec((1, XA_ROWS, D_MODEL), lambda bi, i: (bi, i, 0))
    mixer_specs, mixer_args = [], []
    if mixer is not None:
        a, w_mix = mixer
        K = a.shape[-1]
        mixer_specs = [pl.BlockSpec((1, XA_ROWS, K), lambda bi, i: (bi, i, 0)),
                       _resident((K, D_MODEL)),
                       _resident((1, D_MODEL), (layer, 1)), _resident((1, D_MODEL), (layer, 1))]
        mixer_args = [a, w_mix, ln_g, ln_b]
    return pl.pallas_call(
        functools.partial(_xattn_ln_kernel, mixer_proj=mixer is not None),
        grid=(B, S // XA_ROWS),
        in_specs=[row] + mixer_specs + [
            pl.BlockSpec((None, None, M, 2 * D_MODEL), lambda bi, i: (layer, bi, 0, 0)),
            _resident((D_MODEL, D_MODEL)), _resident((D_MODEL, D_MODEL)),
            _resident((1, D_MODEL), (layer, 2)), _resident((1, D_MODEL), (layer, 2))],
        out_specs=row,
        out_shape=jax.ShapeDtypeStruct((B, S, D_MODEL), F32),
        compiler_params=_params(2),
        name="xattn_ln",
    )(x, *mixer_args, kv_all, wq, wo, ln_g, ln_b)


DA_PROJ_ROWS = 512
DA_QBLOCK = 512
DA_WIDTH = DA_HEADS * DA_V_DIM


def _da_proj_kernel(x_ref, w_ref, cos_ref, sin_ref, q_ref, k_ref, v_ref):
    xb = x_ref[0].astype(BF16)
    cos = cos_ref[0]
    sin = sin_ref[0]
    lane = lax.broadcasted_iota(jnp.int32, (1, DA_V_DIM), 1)
    first_half = (lane % DA_HEAD_DIM) < (DA_HEAD_DIM // 2)
    q_scale = DA_HEAD_DIM ** -0.5 * LOG2E
    for part, out_ref, scale in ((0, q_ref, q_scale), (1, k_ref, 1.0)):
        y = _dot(xb, w_ref[:, part * DA_WIDTH:(part + 1) * DA_WIDTH])
        for h in range(DA_HEADS):
            yh = y[:, h * DA_V_DIM:(h + 1) * DA_V_DIM]
            partner = jnp.where(first_half,
                                pltpu.roll(yh, DA_V_DIM - DA_HEAD_DIM // 2, axis=1),
                                pltpu.roll(yh, DA_HEAD_DIM // 2, axis=1))
            r = yh * cos + partner * sin
            out_ref[h] = (r * scale).astype(BF16)
    v = _dot(xb, w_ref[:, 2 * DA_WIDTH:]).astype(BF16)
    for h in range(DA_HEADS):
        v_ref[h] = v[:, h * DA_V_DIM:(h + 1) * DA_V_DIM]


def _da_proj(x, w_qkv, cos_t, sin_t):
    B, S, _ = x.shape
    row = pl.BlockSpec((1, DA_PROJ_ROWS, D_MODEL), lambda bi, i: (bi, i, 0))
    tab = pl.BlockSpec((1, DA_PROJ_ROWS, LANES), lambda bi, i: (bi, i, 0))
    heads = pl.BlockSpec((None, DA_HEADS, DA_PROJ_ROWS, DA_V_DIM), lambda bi, i: (bi, 0, i, 0))
    out = jax.ShapeDtypeStruct((B, DA_HEADS, S, DA_V_DIM), BF16)
    return pl.pallas_call(
        _da_proj_kernel,
        grid=(B, S // DA_PROJ_ROWS),
        in_specs=[row, _resident((D_MODEL, 3 * DA_WIDTH)), tab, tab],
        out_specs=[heads, heads, heads],
        out_shape=[out, out, out],
        compiler_params=_params(2),
        name="da_proj",
    )(x, w_qkv, cos_t, sin_t)


DA_ONES_ROWS = 16


DA_HEADS_PER_STEP = 4


def _da_attn_kernel(q_ref, k_ref, v_ref, lq_ref, lk_ref, sg_ref, o_ref, vt_ref, *, lam_init):
    G = DA_HEADS_PER_STEP
    S = q_ref.shape[1]
    tq = DA_QBLOCK
    for g in range(G):
        vt_ref[g, :DA_V_DIM, :] = v_ref[g].astype(F32).T.astype(BF16)
        vt_ref[g, DA_V_DIM:, :] = jnp.ones((DA_ONES_ROWS, S), BF16)
    lqk = lq_ref[...] * lk_ref[...]
    lam = (jnp.exp(jnp.sum(lqk[0:1], axis=-1, keepdims=True))
           - jnp.exp(jnp.sum(lqk[1:2], axis=-1, keepdims=True)) + lam_init)
    lane = lax.broadcasted_iota(jnp.int32, (1, DA_V_DIM), 1)
    map0 = lane < DA_HEAD_DIM
    sub_g = sg_ref[...] * (1.0 - lam_init)
    key = lax.broadcasted_iota(jnp.int32, (tq, 2 * tq), 0)
    qry = lax.broadcasted_iota(jnp.int32, (tq, 2 * tq), 1)
    causal = key <= jnp.where(qry >= tq, qry - tq, qry)
    neg = jnp.finfo(F32).min

    def scores(g, i):
        n0 = i * tq
        q = q_ref[g, n0:n0 + tq, :]
        zero = jnp.zeros_like(q)
        q2 = jnp.concatenate([jnp.where(map0, q, zero), jnp.where(map0, zero, q)], axis=0)
        s_d = jnp.where(causal, _dot_nt(k_ref[g, n0:n0 + tq, :], q2), neg)
        s_p = _dot_nt(k_ref[g, :n0, :], q2) if i > 0 else None
        return s_p, s_d

    def finish(g, i, s_p, s_d):
        n0 = i * tq
        m = jnp.max(s_d, axis=0, keepdims=True)
        if i > 0:
            m = jnp.maximum(m, jnp.max(s_p, axis=0, keepdims=True))
            acc = _dot(vt_ref[g, :, :n0], jnp.exp2(s_p - m).astype(BF16))
            acc = acc + _dot(vt_ref[g, :, n0:n0 + tq], jnp.exp2(s_d - m).astype(BF16))
        else:
            acc = _dot(vt_ref[g, :, n0:n0 + tq], jnp.exp2(s_d - m).astype(BF16))
        r = acc[:DA_V_DIM] / acc[DA_V_DIM:DA_V_DIM + 1]
        o = r[:, :tq] - lam * r[:, tq:]
        o = o * lax.rsqrt(jnp.mean(o * o, axis=0, keepdims=True) + LN_EPS) * sub_g
        o_ref[0, n0:n0 + tq, g * DA_V_DIM:(g + 1) * DA_V_DIM] = o.T.astype(BF16)

    items = [(g, i) for i in range(S // tq) for g in range(G)]
    pending = scores(*items[0])
    for n, (g, i) in enumerate(items):
        upcoming = scores(*items[n + 1]) if n + 1 < len(items) else None
        finish(g, i, *pending)
        pending = upcoming


def _da_attn(q, k, v, lam_q, lam_k, subln_g, j, lam_init):
    B, _, S, _ = q.shape
    G = DA_HEADS_PER_STEP
    head = pl.BlockSpec((None, G, S, DA_V_DIM), lambda bi, h: (bi, h, 0, 0))
    return pl.pallas_call(
        functools.partial(_da_attn_kernel, lam_init=lam_init),
        grid=(B, DA_HEADS // G),
        in_specs=[head, head, head, _resident((2, DA_HEAD_DIM), (j,)),
                  _resident((2, DA_HEAD_DIM), (j,)), _resident((DA_V_DIM, 1), (j,))],
        out_specs=pl.BlockSpec((1, S, G * DA_V_DIM), lambda bi, h: (bi, 0, h)),
        out_shape=jax.ShapeDtypeStruct((B, S, DA_WIDTH), BF16),
        scratch_shapes=[pltpu.VMEM((G, DA_V_DIM + DA_ONES_ROWS, S), BF16)],
        compiler_params=_params(2),
        name="da_attn",
    )(q, k, v, lam_q, lam_k, subln_g)


POOL_ROWS = 512
POOL_SKIP = 8
assert POOL_WINDOWS == tuple(2 ** (l + 1) for l in range(len(POOL_WINDOWS)))
assert POOL_HALO >= POOL_SKIP + sum(w // 2 for w in POOL_WINDOWS) and POOL_HALO % 8 == 0


def _pool_ln_kernel(x_ref, prev_ref, w_ref, bias_ref, scale_ref, g_ref, b_ref, o_ref, *bufs,
                    tiles_per_seq):
    i = pl.program_id(0)
    t0 = (i % tiles_per_seq) * POOL_ROWS
    x = x_ref[...]
    first = (i % tiles_per_seq) == 0
    n_rows = POOL_HALO + POOL_ROWS
    bufs[0][0:POOL_HALO, :] = jnp.where(first, 0.0, prev_ref[...])
    bufs[0][POOL_HALO:, :] = x
    t = t0 + lax.broadcasted_iota(jnp.int32, (POOL_ROWS, 1), 0)
    outs = []
    for level, w in enumerate(POOL_WINDOWS):
        src = bufs[level]
        shift = w // 2
        tot = src[POOL_SKIP:n_rows, :] + src[POOL_SKIP - shift:n_rows - shift, :]
        if level + 1 < len(POOL_WINDOWS):
            bufs[level + 1][0:POOL_SKIP, :] = jnp.zeros((POOL_SKIP, tot.shape[1] - POOL_GROUP_DIM), F32)
            bufs[level + 1][POOL_SKIP:, :] = tot[:, POOL_GROUP_DIM:]
        xg = x[:, level * POOL_GROUP_DIM:(level + 1) * POOL_GROUP_DIM]
        cnt = jnp.minimum(t + 1, w).astype(F32)
        pooled = tot[POOL_HALO - POOL_SKIP:, 0:POOL_GROUP_DIM] / cnt - xg
        outs.append(_dot(pooled.astype(BF16), w_ref[level]))
    y = (jnp.concatenate(outs, axis=-1) + bias_ref[...]) * scale_ref[...]
    o_ref[...] = _layer_norm(ALPHA * x + y, g_ref[...], b_ref[...])


def _pool_ln(x2d, seq_len, w, bias_all, scale_all, j, ln_g, ln_b, layer):
    T = x2d.shape[0]
    tiles_per_seq = seq_len // POOL_ROWS
    halo_blocks = POOL_ROWS // POOL_HALO
    return pl.pallas_call(
        functools.partial(_pool_ln_kernel, tiles_per_seq=tiles_per_seq),
        grid=(T // POOL_ROWS,),
        in_specs=[pl.BlockSpec((POOL_ROWS, D_MODEL), lambda i: (i, 0)),
                  pl.BlockSpec((POOL_HALO, D_MODEL),
                               lambda i: (jnp.maximum(i * halo_blocks - 1, 0), 0)),
                  _resident(w.shape),
                  _resident((1, D_MODEL), (j,)), _resident((1, D_MODEL), (j,)),
                  _resident((1, D_MODEL), (layer, 1)), _resident((1, D_MODEL), (layer, 1))],
        out_specs=pl.BlockSpec((POOL_ROWS, D_MODEL), lambda i: (i, 0)),
        out_shape=jax.ShapeDtypeStruct((T, D_MODEL), F32),
        scratch_shapes=[pltpu.VMEM((POOL_HALO + POOL_ROWS, D_MODEL - l * POOL_GROUP_DIM), F32)
                        for l in range(len(POOL_WINDOWS))],
        compiler_params=_params(1),
        name="pool_ln",
    )(x2d, x2d, w, bias_all, scale_all, ln_g, ln_b)


RET_PROJ_ROWS = 512
RET_QK_WIDTH = RET_HEADS * RET_QK_DIM
RET_V_WIDTH = RET_HEADS * RET_V_DIM
RET_STEP_ROWS = 512


def _ret_proj_kernel(x_ref, w_ref, cos_ref, sin_ref, q_ref, k_ref, v_ref, sg_ref):
    xb = x_ref[0].astype(BF16)
    cos = cos_ref[0]
    sin = sin_ref[0]
    half = RET_QK_DIM // 2
    for part, out_ref, scale in ((0, q_ref, 1.0), (1, k_ref, RET_QK_DIM ** -0.5)):
        y = _dot(xb, w_ref[:, part * RET_QK_WIDTH:(part + 1) * RET_QK_WIDTH])
        for h in range(RET_HEADS):
            x1 = y[:, h * RET_QK_DIM:h * RET_QK_DIM + half]
            x2 = y[:, h * RET_QK_DIM + half:(h + 1) * RET_QK_DIM]
            out_ref[0, :, h * RET_QK_DIM:h * RET_QK_DIM + half] = (
                (x1 * cos - x2 * sin) * scale).astype(BF16)
            out_ref[0, :, h * RET_QK_DIM + half:(h + 1) * RET_QK_DIM] = (
                (x2 * cos + x1 * sin) * scale).astype(BF16)
    v0 = 2 * RET_QK_WIDTH
    g0 = v0 + RET_V_WIDTH
    for c in range(0, RET_V_WIDTH, 1024):
        v_ref[0, :, c:c + 1024] = _dot(xb, w_ref[:, v0 + c:v0 + c + 1024]).astype(BF16)
        gate = _dot(xb, w_ref[:, g0 + c:g0 + c + 1024])
        sg_ref[0, :, c:c + 1024] = (gate * jax.nn.sigmoid(gate)).astype(BF16)


def _ret_proj(x, w, cos_t, sin_t):
    B, S, _ = x.shape
    grid_row = lambda width: pl.BlockSpec((1, RET_PROJ_ROWS, width), lambda bi, i: (bi, i, 0))
    qk = jax.ShapeDtypeStruct((B, S, RET_QK_WIDTH), BF16)
    vg = jax.ShapeDtypeStruct((B, S, RET_V_WIDTH), BF16)
    return pl.pallas_call(
        _ret_proj_kernel,
        grid=(B, S // RET_PROJ_ROWS),
        in_specs=[grid_row(D_MODEL), _resident(w.shape),
                  grid_row(LANES), grid_row(LANES)],
        out_specs=[grid_row(RET_QK_WIDTH), grid_row(RET_QK_WIDTH),
                   grid_row(RET_V_WIDTH), grid_row(RET_V_WIDTH)],
        out_shape=[qk, qk, vg, vg],
        compiler_params=_params(2),
        name="ret_proj",
    )(x, w, cos_t, sin_t)


def _ret_core_kernel(q_ref, k_ref, v_ref, sg_ref, o_ref, state_ref):
    C = RET_CHUNK

    @pl.when(pl.program_id(1) == 0)
    def _():
        state_ref[...] = jnp.zeros_like(state_ref)

    ri = lax.broadcasted_iota(jnp.int32, (C, C), 0)
    ci = lax.broadcasted_iota(jnp.int32, (C, C), 1)
    rel = (ri - ci).astype(F32)
    idx = lax.broadcasted_iota(jnp.int32, (C, 1), 0).astype(F32)
    decays = []
    for h in range(RET_HEADS):
        log_gamma = math.log(1.0 - 2.0 ** (-5.0 - h))
        decays.append((jnp.where(rel >= 0, jnp.exp(jnp.maximum(rel, 0.0) * log_gamma), 0.0),
                       jnp.exp((idx + 1.0) * log_gamma),
                       jnp.exp((C - 1.0 - idx) * log_gamma),
                       math.exp(C * log_gamma)))
    for c in range(RET_STEP_ROWS // C):
        rows = slice(c * C, (c + 1) * C)
        for h in range(RET_HEADS):
            d_intra, q_decay, k_decay, chunk_decay = decays[h]
            qs = slice(h * RET_QK_DIM, (h + 1) * RET_QK_DIM)
            vs = slice(h * RET_V_DIM, (h + 1) * RET_V_DIM)
            qc = q_ref[0, rows, qs]
            kc = k_ref[0, rows, qs]
            vc = v_ref[0, rows, vs]
            state = state_ref[h]
            att = _dot_nt(qc, kc) * d_intra
            inner = _dot(att.astype(BF16), vc)
            cross = _dot(qc, state.astype(BF16)) * q_decay
            kd = (kc.astype(F32) * k_decay).astype(BF16)
            state_ref[h] = state * chunk_decay + _dot_tn(kd, vc)
            o = inner + cross
            mu = jnp.mean(o, axis=-1, keepdims=True)
            d = o - mu
            var = jnp.mean(d * d, axis=-1, keepdims=True)
            on = d * lax.rsqrt(var + LN_EPS)
            o_ref[0, rows, vs] = (sg_ref[0, rows, vs].astype(F32) * on).astype(BF16)


def _ret_core(q, k, v, sg):
    B, S, _ = q.shape
    blk = lambda width: pl.BlockSpec((1, RET_STEP_ROWS, width), lambda bi, i: (bi, i, 0))
    return pl.pallas_call(
        _ret_core_kernel,
        grid=(B, S // RET_STEP_ROWS),
        in_specs=[blk(RET_QK_WIDTH), blk(RET_QK_WIDTH), blk(RET_V_WIDTH), blk(RET_V_WIDTH)],
        out_specs=blk(RET_V_WIDTH),
        out_shape=jax.ShapeDtypeStruct((B, S, RET_V_WIDTH), BF16),
        scratch_shapes=[pltpu.VMEM((RET_HEADS, RET_QK_DIM, RET_V_DIM), F32)],
        compiler_params=_params(2),
        name="ret_core",
    )(q, k, v, sg)


def _rope_constants():
    da_inv = 1.0 / (ROPE_THETA ** (jnp.arange(0, DA_HEAD_DIM, 2, dtype=F32) / DA_HEAD_DIM))
    ret_inv = 1.0 / (ROPE_THETA ** jnp.linspace(0.0, 1.0, RET_QK_DIM // 2, dtype=F32))
    half = DA_HEAD_DIM // 2
    da_sign = jnp.concatenate([-jnp.ones(half, F32), jnp.ones(half, F32)] * (LANES // DA_HEAD_DIM))
    return jnp.stack([jnp.tile(da_inv, LANES // half), da_sign, ret_inv, jnp.ones(LANES, F32)])


def kernel(x, mem, positions, ffn_w_in, ffn_w_out, ln_g, ln_b, da_w_qkv, da_w_o, da_lam_q, da_lam_k, da_subln_g, pool_w, pool_b, pool_scale, ret_w_qkvg, ret_w_o, xa_wq, xa_wkv, xa_wo):
    B, S, D = x.shape
    T = B * S
    pos_lanes = jnp.broadcast_to(positions.astype(F32).reshape(T, 1), (T, LANES))
    kv_all = _mem_kv(mem.reshape(-1, D), xa_wkv).reshape(DEPTH, B, mem.shape[1], 2 * D)
    ln_g = ln_g.reshape(DEPTH, 4, 1, D)
    ln_b = ln_b.reshape(DEPTH, 4, 1, D)
    pool_w = pool_w.reshape(-1, D, POOL_GROUP_DIM)
    pool_b = pool_b.reshape(-1, 1, D)
    pool_scale = pool_scale.reshape(-1, 1, D)
    da_subln_g = da_subln_g.reshape(-1, DA_V_DIM, 1)

    def layer_weights(i):
        m, j = i % N_MIXERS, i // N_MIXERS
        mixer = ([("qkv", da_w_qkv, (j,)), ("o", da_w_o, (j,))] if m == 0 else
                 [("pool", pool_w, (j,))] if m == 1 else
                 [("qkvg", ret_w_qkvg, (j,)), ("o", ret_w_o, (j,))])
        return mixer + [("xa_wq", xa_wq, (i,)), ("xa_wo", xa_wo, (i,)),
                        ("w_in", ffn_w_in, (i, 1)), ("w_out", ffn_w_out, (i, 1))]

    def ffn(x, w, i, ln_idx, items, rope=None):
        y, cast, tables = _ffn_ln(x, w["w_in"], w["w_out"], ln_g, ln_b, i, ln_idx,
                                  [(arr, lead) for _, arr, lead in items], rope)
        return y, {name: c for (name, _, _), c in zip(items, cast)}, tables

    w = {"w_in": ffn_w_in[0, 0].astype(BF16), "w_out": ffn_w_out[0, 0].astype(BF16)}
    x = x.reshape(T, D)
    for i in range(DEPTH):
        m, j = i % N_MIXERS, i // N_MIXERS
        x, w, tables = ffn(x, w, i, 0, layer_weights(i),
                           (pos_lanes, _rope_constants()) if i == 0 else None)
        if i == 0:
            da_cos, da_sin, ret_cos, ret_sin = (t.reshape(B, S, LANES) for t in tables)
        mixer = None
        if m == 0:
            lam_init = 0.8 - 0.6 * math.exp(-0.3 * i)
            q, k, v = _da_proj(x.reshape(B, S, D), w["qkv"], da_cos, da_sin)
            mixer = (_da_attn(q, k, v, da_lam_q, da_lam_k, da_subln_g, j, lam_init), w["o"])
        elif m == 1:
            pool_w_bf16 = w["pool"].reshape(len(POOL_WINDOWS), POOL_GROUP_DIM, POOL_GROUP_DIM)
            x = _pool_ln(x, S, pool_w_bf16, pool_b, pool_scale, j, ln_g, ln_b, i)
        else:
            q, k, v, sg = _ret_proj(x.reshape(B, S, D), w["qkvg"], ret_cos, ret_sin)
            mixer = (_ret_core(q, k, v, sg), w["o"])
        x = _xattn_ln(x.reshape(B, S, D), kv_all, w["xa_wq"], w["xa_wo"], ln_g, ln_b, i,
                      mixer).reshape(T, D)
        nxt = ([("w_in", ffn_w_in, (i + 1, 0)), ("w_out", ffn_w_out, (i + 1, 0))]
               if i + 1 < DEPTH else [])
        x, w, _ = ffn(x, w, i, 3, nxt)
    return x.reshape(B, S, D)
```

```python
import functools
import math

import jax
import jax.numpy as jnp
from jax import lax
from jax.experimental import pallas as pl
from jax.experimental.pallas import tpu as pltpu

D_MODEL = 1024
DEPTH = 4
N_MIXERS = 3
DA_HEADS = 8
DA_HEAD_DIM = 64
DA_V_DIM = 2 * DA_HEAD_DIM
ROPE_THETA = 10000.0
POOL_WINDOWS = (2, 4, 8, 16)
POOL_GROUP_DIM = D_MODEL // len(POOL_WINDOWS)
POOL_HALO = 32
RET_HEADS = 4
RET_QK_DIM = D_MODEL // RET_HEADS
RET_V_DIM = 2 * RET_QK_DIM
RET_CHUNK = 128
XA_HEADS = 4
XA_HEAD_DIM = D_MODEL // XA_HEADS
D_FF = 2816
ALPHA = (2 * DEPTH) ** 0.25
LN_EPS = 1e-5
LOG2E = math.log2(math.e)

BF16 = jnp.bfloat16
F32 = jnp.float32

V7X_VMEM_LIMIT_BYTES = 56 * 1024 * 1024
LANES = 128


def _params(n_axes):
    return pltpu.CompilerParams(
        dimension_semantics=("arbitrary",) * n_axes,
        vmem_limit_bytes=V7X_VMEM_LIMIT_BYTES)


def _resident(tail, lead=()):
    shape = (None,) * len(lead) + tuple(tail)
    index = tuple(lead) + (0,) * len(tail)
    return pl.BlockSpec(shape, lambda *_: index, pipeline_mode=pl.Buffered(1))


def _layer_norm(y, g, b):
    mu = jnp.mean(y, axis=-1, keepdims=True)
    d = y - mu
    var = jnp.mean(d * d, axis=-1, keepdims=True)
    return d * lax.rsqrt(var + LN_EPS) * g + b


def _dot(a, b):
    return jnp.dot(a, b, preferred_element_type=F32)


def _dot_nt(a, b):
    return lax.dot_general(a, b, (((1,), (1,)), ((), ())), preferred_element_type=F32)


def _dot_tn(a, b):
    return lax.dot_general(a, b, (((0,), (0,)), ((), ())), preferred_element_type=F32)


def _matmul_kernel(a_ref, w_ref, o_ref):
    o_ref[...] = _dot(a_ref[...].astype(BF16), w_ref[...].astype(BF16)).astype(o_ref.dtype)


def _mem_kv(mem2d, wkv):
    R = mem2d.shape[0]
    L, _, N = wkv.shape
    return pl.pallas_call(
        _matmul_kernel,
        grid=(L,),
        in_specs=[_resident((R, D_MODEL)),
                  pl.BlockSpec((None, D_MODEL, N), lambda l: (l, 0, 0))],
        out_specs=pl.BlockSpec((None, R, N), lambda l: (l, 0, 0)),
        out_shape=jax.ShapeDtypeStruct((L, R, N), BF16),
        compiler_params=_params(1),
        name="mem_kv",
    )(mem2d, wkv)


FFN_ROWS = 1024
FFN_SUB = 512
FFN_CHUNK = 512


BF16_SUBLANES = 16


def _cast_slab_rows(rows, steps):
    return min(r for r in range(BF16_SUBLANES, rows + 1, BF16_SUBLANES)
               if rows % r == 0 and rows // r <= steps)


ROPE_PIECE_ROWS = 128


def _ffn_ln_kernel(*refs, n_cast, n_rope):
    x_ref, win_ref, wout_ref, g_ref, b_ref = refs[:5]
    n_in = 5 + n_cast + (2 if n_rope else 0)
    o_ref = refs[n_in]
    for src_ref, dst_ref in zip(refs[5:5 + n_cast], refs[n_in + 1:n_in + 1 + n_cast]):
        dst_ref[...] = src_ref[...].astype(BF16)
    rope_pieces = []
    if n_rope:
        pos_ref, const_ref = refs[5 + n_cast:n_in]
        tables = refs[n_in + 1 + n_cast:]
        rope_pieces = list(range(0, FFN_ROWS, ROPE_PIECE_ROWS))

    def rope_piece(p0):
        pos = pos_ref[p0:p0 + ROPE_PIECE_ROWS, :]
        for t in range(n_rope):
            ang = pos * const_ref[2 * t:2 * t + 1, :]
            tables[2 * t][p0:p0 + ROPE_PIECE_ROWS, :] = jnp.cos(ang)
            tables[2 * t + 1][p0:p0 + ROPE_PIECE_ROWS, :] = jnp.sin(ang) * const_ref[2 * t + 1:2 * t + 2, :]

    for r0 in range(0, FFN_ROWS, FFN_SUB):
        x = x_ref[r0:r0 + FFN_SUB, :]
        xb = x.astype(BF16)
        acc = jnp.zeros(x.shape, F32)
        for c0 in range(0, D_FF, FFN_CHUNK):
            c1 = min(c0 + FFN_CHUNK, D_FF)
            hg = _dot(xb, win_ref[:, c0:c1])
            hu = _dot(xb, win_ref[:, D_FF + c0:D_FF + c1])
            if rope_pieces:
                rope_piece(rope_pieces.pop(0))
            act = (hg * jax.nn.sigmoid(hg) * hu).astype(BF16)
            acc = acc + _dot(act, wout_ref[c0:c1, :])
        o_ref[r0:r0 + FFN_SUB, :] = _layer_norm(ALPHA * x + 0.5 * acc, g_ref[...], b_ref[...])
    assert not rope_pieces


def _ffn_ln(x2d, w_in, w_out, ln_g, ln_b, layer, ln_idx, to_cast, rope=None):
    T = x2d.shape[0]
    steps = T // FFN_ROWS
    row = pl.BlockSpec((FFN_ROWS, D_MODEL), lambda i: (i, 0))
    rope_in, rope_specs_in, rope_specs_out, rope_shapes = [], [], [], []
    n_rope = 0
    if rope is not None:
        pos_b, consts = rope
        n_rope = consts.shape[0] // 2
        lane_rows = pl.BlockSpec((FFN_ROWS, LANES), lambda i: (i, 0))
        rope_in = [pos_b, consts]
        rope_specs_in = [lane_rows, _resident(consts.shape)]
        rope_specs_out = [lane_rows] * (2 * n_rope)
        rope_shapes = [jax.ShapeDtypeStruct((T, LANES), F32)] * (2 * n_rope)
    cast_specs_in, cast_specs_out, cast_shapes = [], [], []
    for arr, lead in to_cast:
        rows, cols = arr.shape[-2:]
        slab = _cast_slab_rows(rows, steps)
        last = rows // slab - 1
        lead = tuple(lead)
        cast_specs_in.append(pl.BlockSpec(
            (None,) * len(lead) + (slab, cols),
            lambda i, lead=lead, last=last: lead + (jnp.minimum(i, last), 0)))
        cast_specs_out.append(pl.BlockSpec((slab, cols),
                                           lambda i, last=last: (jnp.minimum(i, last), 0)))
        cast_shapes.append(jax.ShapeDtypeStruct((rows, cols), BF16))
    outs = pl.pallas_call(
        functools.partial(_ffn_ln_kernel, n_cast=len(to_cast), n_rope=n_rope),
        grid=(steps,),
        in_specs=[row,
                  _resident((D_MODEL, 2 * D_FF)),
                  _resident((D_FF, D_MODEL)),
                  _resident((1, D_MODEL), (layer, ln_idx)),
                  _resident((1, D_MODEL), (layer, ln_idx))] + cast_specs_in + rope_specs_in,
        out_specs=[row] + cast_specs_out + rope_specs_out,
        out_shape=[jax.ShapeDtypeStruct((T, D_MODEL), F32)] + cast_shapes + rope_shapes,
        compiler_params=_params(1),
        name="ffn_ln",
    )(x2d, w_in, w_out, ln_g, ln_b, *[arr for arr, _ in to_cast], *rope_in)
    n_cast = len(to_cast)
    return outs[0], list(outs[1:1 + n_cast]), list(outs[1 + n_cast:])


XA_ROWS = 1024
XA_SUB = 256


def _xattn_ln_kernel(*refs, mixer_proj):
    x_ref = refs[0]
    kv_ref, wq_ref, wo_ref, g_ref, b_ref, o_ref = refs[-6:]

    def project(r0):
        x = x_ref[0, r0:r0 + XA_SUB, :]
        if mixer_proj:
            a_ref, wm_ref, gm_ref, bm_ref = refs[1:5]
            x = _layer_norm(ALPHA * x + _dot(a_ref[0, r0:r0 + XA_SUB, :], wm_ref[...]),
                            gm_ref[...], bm_ref[...])
        return x, (_dot(x.astype(BF16), wq_ref[...]) * (XA_HEAD_DIM ** -0.5 * LOG2E)).astype(BF16)

    def finish(r0, x, h_out):
        o_ref[0, r0:r0 + XA_SUB, :] = _layer_norm(ALPHA * x + h_out, g_ref[...], b_ref[...])

    starts = list(range(0, XA_ROWS, XA_SUB))
    x, q = project(starts[0])
    lagged = None
    for n, r0 in enumerate(starts):
        s = [_dot_nt(q[:, h * XA_HEAD_DIM:(h + 1) * XA_HEAD_DIM],
                     kv_ref[:, h * XA_HEAD_DIM:(h + 1) * XA_HEAD_DIM]) for h in range(XA_HEADS)]
        upcoming = project(starts[n + 1]) if n + 1 < len(starts) else None
        heads = []
        for h in range(XA_HEADS):
            e = jnp.exp2(s[h] - jnp.max(s[h], axis=-1, keepdims=True))
            p = e * (1.0 / jnp.sum(e, axis=-1, keepdims=True))
            lo = D_MODEL + h * XA_HEAD_DIM
            heads.append(_dot(p.astype(BF16), kv_ref[:, lo:lo + XA_HEAD_DIM]).astype(BF16))
        if lagged is not None:
            finish(*lagged)
        lagged = (r0, x, _dot(jnp.concatenate(heads, axis=-1), wo_ref[...]))
        if upcoming is not None:
            x, q = upcoming
    finish(*lagged)


def _xattn_ln(x, kv_all, wq, wo, ln_g, ln_b, layer, mixer=None):
    B, S, _ = x.shape
    M = kv_all.shape[2]
    row = pl.BlockSpec((1, XA_ROWS, D_MODEL), lambda bi, i: (bi, i, 0))
    mixer_specs, mixer_args = [], []
    if mixer is not None:
        a, w_mix = mixer
        K = a.shape[-1]
        mixer_specs = [pl.BlockSpec((1, XA_ROWS, K), lambda bi, i: (bi, i, 0)),
                       _resident((K, D_MODEL)),
                       _resident((1, D_MODEL), (layer, 1)), _resident((1, D_MODEL), (layer, 1))]
        mixer_args = [a, w_mix, ln_g, ln_b]
    return pl.pallas_call(
        functools.partial(_xattn_ln_kernel, mixer_proj=mixer is not None),
        grid=(B, S // XA_ROWS),
        in_specs=[row] + mixer_specs + [
            pl.BlockSpec((None, None, M, 2 * D_MODEL), lambda bi, i: (layer, bi, 0, 0)),
            _resident((D_MODEL, D_MODEL)), _resident((D_MODEL, D_MODEL)),
            _resident((1, D_MODEL), (layer, 2)), _resident((1, D_MODEL), (layer, 2))],
        out_specs=row,
        out_shape=jax.ShapeDtypeStruct((B, S, D_MODEL), F32),
        compiler_params=_params(2),
        name="xattn_ln",
    )(x, *mixer_args, kv_all, wq, wo, ln_g, ln_b)


DA_PROJ_ROWS = 512
DA_QBLOCK = 512
DA_WIDTH = DA_HEADS * DA_V_DIM


def _da_proj_kernel(x_ref, w_ref, cos_ref, sin_ref, q_ref, k_ref, v_ref):
    xb = x_ref[0].astype(BF16)
    cos = cos_ref[0]
    sin = sin_ref[0]
    lane = lax.broadcasted_iota(jnp.int32, (1, DA_V_DIM), 1)
    first_half = (lane % DA_HEAD_DIM) < (DA_HEAD_DIM // 2)
    q_scale = DA_HEAD_DIM ** -0.5 * LOG2E
    for part, out_ref, scale in ((0, q_ref, q_scale), (1, k_ref, 1.0)):
        y = _dot(xb, w_ref[:, part * DA_WIDTH:(part + 1) * DA_WIDTH])
        for h in range(DA_HEADS):
            yh = y[:, h * DA_V_DIM:(h + 1) * DA_V_DIM]
            partner = jnp.where(first_half,
                                pltpu.roll(yh, DA_V_DIM - DA_HEAD_DIM // 2, axis=1),
                                pltpu.roll(yh, DA_HEAD_DIM // 2, axis=1))
            r = yh * cos + partner * sin
            out_ref[h] = (r * scale).astype(BF16)
    v = _dot(xb, w_ref[:, 2 * DA_WIDTH:]).astype(BF16)
    for h in range(DA_HEADS):
        v_ref[h] = v[:, h * DA_V_DIM:(h + 1) * DA_V_DIM]


def _da_proj(x, w_qkv, cos_t, sin_t):
    B, S, _ = x.shape
    row = pl.BlockSpec((1, DA_PROJ_ROWS, D_MODEL), lambda bi, i: (bi, i, 0))
    tab = pl.BlockSpec((1, DA_PROJ_ROWS, LANES), lambda bi, i: (bi, i, 0))
    heads = pl.BlockSpec((None, DA_HEADS, DA_PROJ_ROWS, DA_V_DIM), lambda bi, i: (bi, 0, i, 0))
    out = jax.ShapeDtypeStruct((B, DA_HEADS, S, DA_V_DIM), BF16)
    return pl.pallas_call(
        _da_proj_kernel,
        grid=(B, S // DA_PROJ_ROWS),
        in_specs=[row, _resident((D_MODEL, 3 * DA_WIDTH)), tab, tab],
        out_specs=[heads, heads, heads],
        out_shape=[out, out, out],
        compiler_params=_params(2),
        name="da_proj",
    )(x, w_qkv, cos_t, sin_t)


DA_ONES_ROWS = 16


DA_HEADS_PER_STEP = 4


def _da_attn_kernel(q_ref, k_ref, v_ref, lq_ref, lk_ref, sg_ref, o_ref, vt_ref, *, lam_init):
    G = DA_HEADS_PER_STEP
    S = q_ref.shape[1]
    tq = DA_QBLOCK
    for g in range(G):
        vt_ref[g, :DA_V_DIM, :] = v_ref[g].astype(F32).T.astype(BF16)
        vt_ref[g, DA_V_DIM:, :] = jnp.ones((DA_ONES_ROWS, S), BF16)
    lqk = lq_ref[...] * lk_ref[...]
    lam = (jnp.exp(jnp.sum(lqk[0:1], axis=-1, keepdims=True))
           - jnp.exp(jnp.sum(lqk[1:2], axis=-1, keepdims=True)) + lam_init)
    lane = lax.broadcasted_iota(jnp.int32, (1, DA_V_DIM), 1)
    map0 = lane < DA_HEAD_DIM
    sub_g = sg_ref[...] * (1.0 - lam_init)
    th = tq // 2
    key = lax.broadcasted_iota(jnp.int32, (th, 2 * th), 0)
    qry = lax.broadcasted_iota(jnp.int32, (th, 2 * th), 1)
    tri = key <= jnp.where(qry >= th, qry - th, qry)
    neg = jnp.finfo(F32).min

    def both_maps(q):
        zero = jnp.zeros_like(q)
        return jnp.concatenate([jnp.where(map0, q, zero), jnp.where(map0, zero, q)], axis=0)

    def scores(g, i):
        n0 = i * tq
        qa = both_maps(q_ref[g, n0:n0 + th, :])
        qb = both_maps(q_ref[g, n0 + th:n0 + tq, :])
        ka = k_ref[g, n0:n0 + th, :]
        kb = k_ref[g, n0 + th:n0 + tq, :]
        s_aa = jnp.where(tri, _dot_nt(ka, qa), neg)
        s_ab = _dot_nt(ka, qb)
        s_bb = jnp.where(tri, _dot_nt(kb, qb), neg)
        s_pa = _dot_nt(k_ref[g, :n0, :], qa) if i > 0 else None
        s_pb = _dot_nt(k_ref[g, :n0, :], qb) if i > 0 else None
        return s_pa, s_pb, s_aa, s_ab, s_bb

    def finish(g, i, s_pa, s_pb, s_aa, s_ab, s_bb):
        n0 = i * tq
        m_a = jnp.max(s_aa, axis=0, keepdims=True)
        m_b = jnp.maximum(jnp.max(s_ab, axis=0, keepdims=True), jnp.max(s_bb, axis=0, keepdims=True))
        if i > 0:
            m_a = jnp.maximum(m_a, jnp.max(s_pa, axis=0, keepdims=True))
            m_b = jnp.maximum(m_b, jnp.max(s_pb, axis=0, keepdims=True))
        acc_a = _dot(vt_ref[g, :, n0:n0 + th], jnp.exp2(s_aa - m_a).astype(BF16))
        acc_b = (_dot(vt_ref[g, :, n0:n0 + th], jnp.exp2(s_ab - m_b).astype(BF16))
                 + _dot(vt_ref[g, :, n0 + th:n0 + tq], jnp.exp2(s_bb - m_b).astype(BF16)))
        if i > 0:
            acc_a = acc_a + _dot(vt_ref[g, :, :n0], jnp.exp2(s_pa - m_a).astype(BF16))
            acc_b = acc_b + _dot(vt_ref[g, :, :n0], jnp.exp2(s_pb - m_b).astype(BF16))
        for half, acc in ((0, acc_a), (1, acc_b)):
            r = acc[:DA_V_DIM] / acc[DA_V_DIM:DA_V_DIM + 1]
            o = r[:, :th] - lam * r[:, th:]
            o = o * lax.rsqrt(jnp.mean(o * o, axis=0, keepdims=True) + LN_EPS) * sub_g
            r0 = n0 + half * th
            o_ref[0, r0:r0 + th, g * DA_V_DIM:(g + 1) * DA_V_DIM] = o.T.astype(BF16)

    items = [(g, i) for i in range(S // tq) for g in range(G)]
    pending = scores(*items[0])
    for n, (g, i) in enumerate(items):
        upcoming = scores(*items[n + 1]) if n + 1 < len(items) else None
        finish(g, i, *pending)
        pending = upcoming


def _da_attn(q, k, v, lam_q, lam_k, subln_g, j, lam_init):
    B, _, S, _ = q.shape
    G = DA_HEADS_PER_STEP
    head = pl.BlockSpec((None, G, S, DA_V_DIM), lambda bi, h: (bi, h, 0, 0))
    return pl.pallas_call(
        functools.partial(_da_attn_kernel, lam_init=lam_init),
        grid=(B, DA_HEADS // G),
        in_specs=[head, head, head, _resident((2, DA_HEAD_DIM), (j,)),
                  _resident((2, DA_HEAD_DIM), (j,)), _resident((DA_V_DIM, 1), (j,))],
        out_specs=pl.BlockSpec((1, S, G * DA_V_DIM), lambda bi, h: (bi, 0, h)),
        out_shape=jax.ShapeDtypeStruct((B, S, DA_WIDTH), BF16),
        scratch_shapes=[pltpu.VMEM((G, DA_V_DIM + DA_ONES_ROWS, S), BF16)],
        compiler_params=_params(2),
        name="da_attn",
    )(q, k, v, lam_q, lam_k, subln_g)


POOL_ROWS = 512
POOL_SKIP = 8
assert POOL_WINDOWS == tuple(2 ** (l + 1) for l in range(len(POOL_WINDOWS)))
assert POOL_HALO >= POOL_SKIP + sum(w // 2 for w in POOL_WINDOWS) and POOL_HALO % 8 == 0


def _pool_ln_kernel(x_ref, prev_ref, w_ref, bias_ref, scale_ref, g_ref, b_ref, o_ref, *bufs,
                    tiles_per_seq):
    i = pl.program_id(0)
    t0 = (i % tiles_per_seq) * POOL_ROWS
    x = x_ref[...]
    first = (i % tiles_per_seq) == 0
    n_rows = POOL_HALO + POOL_ROWS
    bufs[0][0:POOL_HALO, :] = jnp.where(first, 0.0, prev_ref[...])
    bufs[0][POOL_HALO:, :] = x
    t = t0 + lax.broadcasted_iota(jnp.int32, (POOL_ROWS, 1), 0)
    outs = []
    for level, w in enumerate(POOL_WINDOWS):
        src = bufs[level]
        shift = w // 2
        tot = src[POOL_SKIP:n_rows, :] + src[POOL_SKIP - shift:n_rows - shift, :]
        if level + 1 < len(POOL_WINDOWS):
            bufs[level + 1][0:POOL_SKIP, :] = jnp.zeros((POOL_SKIP, tot.shape[1] - POOL_GROUP_DIM), F32)
            bufs[level + 1][POOL_SKIP:, :] = tot[:, POOL_GROUP_DIM:]
        xg = x[:, level * POOL_GROUP_DIM:(level + 1) * POOL_GROUP_DIM]
        cnt = jnp.minimum(t + 1, w).astype(F32)
        pooled = tot[POOL_HALO - POOL_SKIP:, 0:POOL_GROUP_DIM] / cnt - xg
        outs.append(_dot(pooled.astype(BF16), w_ref[level]))
    y = (jnp.concatenate(outs, axis=-1) + bias_ref[...]) * scale_ref[...]
    o_ref[...] = _layer_norm(ALPHA * x + y, g_ref[...], b_ref[...])


def _pool_ln(x2d, seq_len, w, bias_all, scale_all, j, ln_g, ln_b, layer):
    T = x2d.shape[0]
    tiles_per_seq = seq_len // POOL_ROWS
    halo_blocks = POOL_ROWS // POOL_HALO
    return pl.pallas_call(
        functools.partial(_pool_ln_kernel, tiles_per_seq=tiles_per_seq),
        grid=(T // POOL_ROWS,),
        in_specs=[pl.BlockSpec((POOL_ROWS, D_MODEL), lambda i: (i, 0)),
                  pl.BlockSpec((POOL_HALO, D_MODEL),
                               lambda i: (jnp.maximum(i * halo_blocks - 1, 0), 0)),
                  _resident(w.shape),
                  _resident((1, D_MODEL), (j,)), _resident((1, D_MODEL), (j,)),
                  _resident((1, D_MODEL), (layer, 1)), _resident((1, D_MODEL), (layer, 1))],
        out_specs=pl.BlockSpec((POOL_ROWS, D_MODEL), lambda i: (i, 0)),
        out_shape=jax.ShapeDtypeStruct((T, D_MODEL), F32),
        scratch_shapes=[pltpu.VMEM((POOL_HALO + POOL_ROWS, D_MODEL - l * POOL_GROUP_DIM), F32)
                        for l in range(len(POOL_WINDOWS))],
        compiler_params=_params(1),
        name="pool_ln",
    )(x2d, x2d, w, bias_all, scale_all, ln_g, ln_b)


RET_PROJ_ROWS = 512
RET_QK_WIDTH = RET_HEADS * RET_QK_DIM
RET_V_WIDTH = RET_HEADS * RET_V_DIM
RET_STEP_ROWS = 512


def _ret_proj_kernel(x_ref, w_ref, cos_ref, sin_ref, q_ref, k_ref, v_ref, sg_ref):
    xb = x_ref[0].astype(BF16)
    cos = cos_ref[0]
    sin = sin_ref[0]
    half = RET_QK_DIM // 2
    for part, out_ref, scale in ((0, q_ref, 1.0), (1, k_ref, RET_QK_DIM ** -0.5)):
        y = _dot(xb, w_ref[:, part * RET_QK_WIDTH:(part + 1) * RET_QK_WIDTH])
        for h in range(RET_HEADS):
            x1 = y[:, h * RET_QK_DIM:h * RET_QK_DIM + half]
            x2 = y[:, h * RET_QK_DIM + half:(h + 1) * RET_QK_DIM]
            out_ref[0, :, h * RET_QK_DIM:h * RET_QK_DIM + half] = (
                (x1 * cos - x2 * sin) * scale).astype(BF16)
            out_ref[0, :, h * RET_QK_DIM + half:(h + 1) * RET_QK_DIM] = (
                (x2 * cos + x1 * sin) * scale).astype(BF16)
    v0 = 2 * RET_QK_WIDTH
    g0 = v0 + RET_V_WIDTH
    for c in range(0, RET_V_WIDTH, 1024):
        v_ref[0, :, c:c + 1024] = _dot(xb, w_ref[:, v0 + c:v0 + c + 1024]).astype(BF16)
        gate = _dot(xb, w_ref[:, g0 + c:g0 + c + 1024])
        sg_ref[0, :, c:c + 1024] = (gate * jax.nn.sigmoid(gate)).astype(BF16)


def _ret_proj(x, w, cos_t, sin_t):
    B, S, _ = x.shape
    grid_row = lambda width: pl.BlockSpec((1, RET_PROJ_ROWS, width), lambda bi, i: (bi, i, 0))
    qk = jax.ShapeDtypeStruct((B, S, RET_QK_WIDTH), BF16)
    vg = jax.ShapeDtypeStruct((B, S, RET_V_WIDTH), BF16)
    return pl.pallas_call(
        _ret_proj_kernel,
        grid=(B, S // RET_PROJ_ROWS),
        in_specs=[grid_row(D_MODEL), _resident(w.shape),
                  grid_row(LANES), grid_row(LANES)],
        out_specs=[grid_row(RET_QK_WIDTH), grid_row(RET_QK_WIDTH),
                   grid_row(RET_V_WIDTH), grid_row(RET_V_WIDTH)],
        out_shape=[qk, qk, vg, vg],
        compiler_params=_params(2),
        name="ret_proj",
    )(x, w, cos_t, sin_t)


def _ret_core_kernel(q_ref, k_ref, v_ref, sg_ref, o_ref, state_ref):
    C = RET_CHUNK

    @pl.when(pl.program_id(1) == 0)
    def _():
        state_ref[...] = jnp.zeros_like(state_ref)

    ri = lax.broadcasted_iota(jnp.int32, (C, C), 0)
    ci = lax.broadcasted_iota(jnp.int32, (C, C), 1)
    rel = (ri - ci).astype(F32)
    idx = lax.broadcasted_iota(jnp.int32, (C, 1), 0).astype(F32)
    decays = []
    for h in range(RET_HEADS):
        log_gamma = math.log(1.0 - 2.0 ** (-5.0 - h))
        decays.append((jnp.where(rel >= 0, jnp.exp(jnp.maximum(rel, 0.0) * log_gamma), 0.0),
                       jnp.exp((idx + 1.0) * log_gamma),
                       jnp.exp((C - 1.0 - idx) * log_gamma),
                       math.exp(C * log_gamma)))
    for c in range(RET_STEP_ROWS // C):
        rows = slice(c * C, (c + 1) * C)
        for h in range(RET_HEADS):
            d_intra, q_decay, k_decay, chunk_decay = decays[h]
            qs = slice(h * RET_QK_DIM, (h + 1) * RET_QK_DIM)
            vs = slice(h * RET_V_DIM, (h + 1) * RET_V_DIM)
            qc = q_ref[0, rows, qs]
            kc = k_ref[0, rows, qs]
            vc = v_ref[0, rows, vs]
            state = state_ref[h]
            att = _dot_nt(qc, kc) * d_intra
            inner = _dot(att.astype(BF16), vc)
            cross = _dot(qc, state.astype(BF16)) * q_decay
            kd = (kc.astype(F32) * k_decay).astype(BF16)
            state_ref[h] = state * chunk_decay + _dot_tn(kd, vc)
            o = inner + cross
            mu = jnp.mean(o, axis=-1, keepdims=True)
            d = o - mu
            var = jnp.mean(d * d, axis=-1, keepdims=True)
            on = d * lax.rsqrt(var + LN_EPS)
            o_ref[0, rows, vs] = (sg_ref[0, rows, vs].astype(F32) * on).astype(BF16)


def _ret_core(q, k, v, sg):
    B, S, _ = q.shape
    blk = lambda width: pl.BlockSpec((1, RET_STEP_ROWS, width), lambda bi, i: (bi, i, 0))
    return pl.pallas_call(
        _ret_core_kernel,
        grid=(B, S // RET_STEP_ROWS),
        in_specs=[blk(RET_QK_WIDTH), blk(RET_QK_WIDTH), blk(RET_V_WIDTH), blk(RET_V_WIDTH)],
        out_specs=blk(RET_V_WIDTH),
        out_shape=jax.ShapeDtypeStruct((B, S, RET_V_WIDTH), BF16),
        scratch_shapes=[pltpu.VMEM((RET_HEADS, RET_QK_DIM, RET_V_DIM), F32)],
        compiler_params=_params(2),
        name="ret_core",
    )(q, k, v, sg)


def _rope_constants():
    da_inv = 1.0 / (ROPE_THETA ** (jnp.arange(0, DA_HEAD_DIM, 2, dtype=F32) / DA_HEAD_DIM))
    ret_inv = 1.0 / (ROPE_THETA ** jnp.linspace(0.0, 1.0, RET_QK_DIM // 2, dtype=F32))
    half = DA_HEAD_DIM // 2
    da_sign = jnp.concatenate([-jnp.ones(half, F32), jnp.ones(half, F32)] * (LANES // DA_HEAD_DIM))
    return jnp.stack([jnp.tile(da_inv, LANES // half), da_sign, ret_inv, jnp.ones(LANES, F32)])


def kernel(x, mem, positions, ffn_w_in, ffn_w_out, ln_g, ln_b, da_w_qkv, da_w_o, da_lam_q, da_lam_k, da_subln_g, pool_w, pool_b, pool_scale, ret_w_qkvg, ret_w_o, xa_wq, xa_wkv, xa_wo):
    B, S, D = x.shape
    T = B * S
    pos_lanes = jnp.broadcast_to(positions.astype(F32).reshape(T, 1), (T, LANES))
    kv_all = _mem_kv(mem.reshape(-1, D), xa_wkv).reshape(DEPTH, B, mem.shape[1], 2 * D)
    ln_g = ln_g.reshape(DEPTH, 4, 1, D)
    ln_b = ln_b.reshape(DEPTH, 4, 1, D)
    pool_w = pool_w.reshape(-1, D, POOL_GROUP_DIM)
    pool_b = pool_b.reshape(-1, 1, D)
    pool_scale = pool_scale.reshape(-1, 1, D)
    da_subln_g = da_subln_g.reshape(-1, DA_V_DIM, 1)

    def layer_weights(i):
        m, j = i % N_MIXERS, i // N_MIXERS
        mixer = ([("qkv", da_w_qkv, (j,)), ("o", da_w_o, (j,))] if m == 0 else
                 [("pool", pool_w, (j,))] if m == 1 else
                 [("qkvg", ret_w_qkvg, (j,)), ("o", ret_w_o, (j,))])
        return mixer + [("xa_wq", xa_wq, (i,)), ("xa_wo", xa_wo, (i,)),
                        ("w_in", ffn_w_in, (i, 1)), ("w_out", ffn_w_out, (i, 1))]

    def ffn(x, w, i, ln_idx, items, rope=None):
        y, cast, tables = _ffn_ln(x, w["w_in"], w["w_out"], ln_g, ln_b, i, ln_idx,
                                  [(arr, lead) for _, arr, lead in items], rope)
        return y, {name: c for (name, _, _), c in zip(items, cast)}, tables

    w = {"w_in": ffn_w_in[0, 0].astype(BF16), "w_out": ffn_w_out[0, 0].astype(BF16)}
    x = x.reshape(T, D)
    for i in range(DEPTH):
        m, j = i % N_MIXERS, i // N_MIXERS
        x, w, tables = ffn(x, w, i, 0, layer_weights(i),
                           (pos_lanes, _rope_constants()) if i == 0 else None)
        if i == 0:
            da_cos, da_sin, ret_cos, ret_sin = (t.reshape(B, S, LANES) for t in tables)
        mixer = None
        if m == 0:
            lam_init = 0.8 - 0.6 * math.exp(-0.3 * i)
            q, k, v = _da_proj(x.reshape(B, S, D), w["qkv"], da_cos, da_sin)
            mixer = (_da_attn(q, k, v, da_lam_q, da_lam_k, da_subln_g, j, lam_init), w["o"])
        elif m == 1:
            pool_w_bf16 = w["pool"].reshape(len(POOL_WINDOWS), POOL_GROUP_DIM, POOL_GROUP_DIM)
            x = _pool_ln(x, S, pool_w_bf16, pool_b, pool_scale, j, ln_g, ln_b, i)
        else:
            q, k, v, sg = _ret_proj(x.reshape(B, S, D), w["qkvg"], ret_cos, ret_sin)
            mixer = (_ret_core(q, k, v, sg), w["o"])
        x = _xattn_ln(x.reshape(B, S, D), kv_all, w["xa_wq"], w["xa_wo"], ln_g, ln_b, i,
                      mixer).reshape(T, D)
        nxt = ([("w_in", ffn_w_in, (i + 1, 0)), ("w_out", ffn_w_out, (i + 1, 0))]
               if i + 1 < DEPTH else [])
        x, w, _ = ffn(x, w, i, 3, nxt)
    return x.reshape(B, S, D)
```

```python
import functools
import math

import jax
import jax.numpy as jnp
from jax import lax
from jax.experimental import pallas as pl
from jax.experimental.pallas import tpu as pltpu

D_MODEL = 1024
DEPTH = 4
N_MIXERS = 3
DA_HEADS = 8
DA_HEAD_DIM = 64
DA_V_DIM = 2 * DA_HEAD_DIM
ROPE_THETA = 10000.0
POOL_WINDOWS = (2, 4, 8, 16)
POOL_GROUP_DIM = D_MODEL // len(POOL_WINDOWS)
POOL_HALO = 32
RET_HEADS = 4
RET_QK_DIM = D_MODEL // RET_HEADS
RET_V_DIM = 2 * RET_QK_DIM
RET_CHUNK = 256
XA_HEADS = 4
XA_HEAD_DIM = D_MODEL // XA_HEADS
D_FF = 2816
ALPHA = (2 * DEPTH) ** 0.25
LN_EPS = 1e-5
LOG2E = math.log2(math.e)

BF16 = jnp.bfloat16
F32 = jnp.float32

V7X_VMEM_LIMIT_BYTES = 56 * 1024 * 1024
LANES = 128


def _params(n_axes):
    return pltpu.CompilerParams(
        dimension_semantics=("arbitrary",) * n_axes,
        vmem_limit_bytes=V7X_VMEM_LIMIT_BYTES)


def _resident(tail, lead=()):
    shape = (None,) * len(lead) + tuple(tail)
    index = tuple(lead) + (0,) * len(tail)
    return pl.BlockSpec(shape, lambda *_: index, pipeline_mode=pl.Buffered(1))


def _layer_norm(y, g, b):
    mu = jnp.mean(y, axis=-1, keepdims=True)
    d = y - mu
    var = jnp.mean(d * d, axis=-1, keepdims=True)
    return d * lax.rsqrt(var + LN_EPS) * g + b


def _dot(a, b):
    return jnp.dot(a, b, preferred_element_type=F32)


def _dot_nt(a, b):
    return lax.dot_general(a, b, (((1,), (1,)), ((), ())), preferred_element_type=F32)


def _dot_tn(a, b):
    return lax.dot_general(a, b, (((0,), (0,)), ((), ())), preferred_element_type=F32)


def _matmul_kernel(a_ref, w_ref, o_ref):
    o_ref[...] = _dot(a_ref[...].astype(BF16), w_ref[...].astype(BF16)).astype(o_ref.dtype)


def _mem_kv(mem2d, wkv):
    R = mem2d.shape[0]
    L, _, N = wkv.shape
    return pl.pallas_call(
        _matmul_kernel,
        grid=(L,),
        in_specs=[_resident((R, D_MODEL)),
                  pl.BlockSpec((None, D_MODEL, N), lambda l: (l, 0, 0))],
        out_specs=pl.BlockSpec((None, R, N), lambda l: (l, 0, 0)),
        out_shape=jax.ShapeDtypeStruct((L, R, N), BF16),
        compiler_params=_params(1),
        name="mem_kv",
    )(mem2d, wkv)


FFN_ROWS = 1024
FFN_SUB = 512
FFN_CHUNK = 256


BF16_SUBLANES = 16


def _cast_slab_rows(rows, steps):
    return min(r for r in range(BF16_SUBLANES, rows + 1, BF16_SUBLANES)
               if rows % r == 0 and rows // r <= steps)


ROPE_PIECE_ROWS = 128


def _ffn_ln_kernel(*refs, n_cast, n_rope):
    x_ref, win_ref, wout_ref, g_ref, b_ref = refs[:5]
    n_in = 5 + n_cast + (2 if n_rope else 0)
    o_ref = refs[n_in]
    for src_ref, dst_ref in zip(refs[5:5 + n_cast], refs[n_in + 1:n_in + 1 + n_cast]):
        dst_ref[...] = src_ref[...].astype(BF16)
    rope_pieces = []
    if n_rope:
        pos_ref, const_ref = refs[5 + n_cast:n_in]
        tables = refs[n_in + 1 + n_cast:]
        rope_pieces = list(range(0, FFN_ROWS, ROPE_PIECE_ROWS))

    def rope_piece(p0):
        pos = pos_ref[p0:p0 + ROPE_PIECE_ROWS, :]
        for t in range(n_rope):
            ang = pos * const_ref[2 * t:2 * t + 1, :]
            tables[2 * t][p0:p0 + ROPE_PIECE_ROWS, :] = jnp.cos(ang)
            tables[2 * t + 1][p0:p0 + ROPE_PIECE_ROWS, :] = jnp.sin(ang) * const_ref[2 * t + 1:2 * t + 2, :]

    for r0 in range(0, FFN_ROWS, FFN_SUB):
        x = x_ref[r0:r0 + FFN_SUB, :]
        xb = x.astype(BF16)
        acc = jnp.zeros(x.shape, F32)
        for c0 in range(0, D_FF, FFN_CHUNK):
            c1 = min(c0 + FFN_CHUNK, D_FF)
            hg = _dot(xb, win_ref[:, c0:c1])
            hu = _dot(xb, win_ref[:, D_FF + c0:D_FF + c1])
            if rope_pieces:
                rope_piece(rope_pieces.pop(0))
            act = (hg * jax.nn.sigmoid(hg) * hu).astype(BF16)
            acc = acc + _dot(act, wout_ref[c0:c1, :])
        o_ref[r0:r0 + FFN_SUB, :] = _layer_norm(ALPHA * x + 0.5 * acc, g_ref[...], b_ref[...])
    assert not rope_pieces


def _ffn_ln(x2d, w_in, w_out, ln_g, ln_b, layer, ln_idx, to_cast, rope=None):
    T = x2d.shape[0]
    steps = T // FFN_ROWS
    row = pl.BlockSpec((FFN_ROWS, D_MODEL), lambda i: (i, 0))
    rope_in, rope_specs_in, rope_specs_out, rope_shapes = [], [], [], []
    n_rope = 0
    if rope is not None:
        pos_b, consts = rope
        n_rope = consts.shape[0] // 2
        lane_rows = pl.BlockSpec((FFN_ROWS, LANES), lambda i: (i, 0))
        rope_in = [pos_b, consts]
        rope_specs_in = [lane_rows, _resident(consts.shape)]
        rope_specs_out = [lane_rows] * (2 * n_rope)
        rope_shapes = [jax.ShapeDtypeStruct((T, LANES), F32)] * (2 * n_rope)
    cast_specs_in, cast_specs_out, cast_shapes = [], [], []
    for arr, lead in to_cast:
        rows, cols = arr.shape[-2:]
        slab = _cast_slab_rows(rows, steps)
        last = rows // slab - 1
        lead = tuple(lead)
        cast_specs_in.append(pl.BlockSpec(
            (None,) * len(lead) + (slab, cols),
            lambda i, lead=lead, last=last: lead + (jnp.minimum(i, last), 0)))
        cast_specs_out.append(pl.BlockSpec((slab, cols),
                                           lambda i, last=last: (jnp.minimum(i, last), 0)))
        cast_shapes.append(jax.ShapeDtypeStruct((rows, cols), BF16))
    outs = pl.pallas_call(
        functools.partial(_ffn_ln_kernel, n_cast=len(to_cast), n_rope=n_rope),
        grid=(steps,),
        in_specs=[row,
                  _resident((D_MODEL, 2 * D_FF)),
                  _resident((D_FF, D_MODEL)),
                  _resident((1, D_MODEL), (layer, ln_idx)),
                  _resident((1, D_MODEL), (layer, ln_idx))] + cast_specs_in + rope_specs_in,
        out_specs=[row] + cast_specs_out + rope_specs_out,
        out_shape=[jax.ShapeDtypeStruct((T, D_MODEL), F32)] + cast_shapes + rope_shapes,
        compiler_params=_params(1),
        name="ffn_ln",
    )(x2d, w_in, w_out, ln_g, ln_b, *[arr for arr, _ in to_cast], *rope_in)
    n_cast = len(to_cast)
    return outs[0], list(outs[1:1 + n_cast]), list(outs[1 + n_cast:])


XA_ROWS = 1024
XA_SUB = 256


def _xattn_ln_kernel(*refs, mixer_proj):
    x_ref = refs[0]
    kv_ref, wq_ref, wo_ref, g_ref, b_ref, o_ref = refs[-6:]

    def project(r0):
        x = x_ref[0, r0:r0 + XA_SUB, :]
        if mixer_proj:
            a_ref, wm_ref, gm_ref, bm_ref = refs[1:5]
            x = _layer_norm(ALPHA * x + _dot(a_ref[0, r0:r0 + XA_SUB, :], wm_ref[...]),
                            gm_ref[...], bm_ref[...])
        return x, (_dot(x.astype(BF16), wq_ref[...]) * (XA_HEAD_DIM ** -0.5 * LOG2E)).astype(BF16)

    def finish(r0, x, h_out):
        o_ref[0, r0:r0 + XA_SUB, :] = _layer_norm(ALPHA * x + h_out, g_ref[...], b_ref[...])

    starts = list(range(0, XA_ROWS, XA_SUB))
    x, q = project(starts[0])
    lagged = None
    for n, r0 in enumerate(starts):
        s = [_dot_nt(q[:, h * XA_HEAD_DIM:(h + 1) * XA_HEAD_DIM],
                     kv_ref[:, h * XA_HEAD_DIM:(h + 1) * XA_HEAD_DIM]) for h in range(XA_HEADS)]
        upcoming = project(starts[n + 1]) if n + 1 < len(starts) else None
        heads = []
        for h in range(XA_HEADS):
            e = jnp.exp2(s[h] - jnp.max(s[h], axis=-1, keepdims=True))
            p = e * (1.0 / jnp.sum(e, axis=-1, keepdims=True))
            lo = D_MODEL + h * XA_HEAD_DIM
            heads.append(_dot(p.astype(BF16), kv_ref[:, lo:lo + XA_HEAD_DIM]).astype(BF16))
        if lagged is not None:
            finish(*lagged)
        lagged = (r0, x, _dot(jnp.concatenate(heads, axis=-1), wo_ref[...]))
        if upcoming is not None:
            x, q = upcoming
    finish(*lagged)


def _xattn_ln(x, kv_all, wq, wo, ln_g, ln_b, layer, mixer=None):
    B, S, _ = x.shape
    M = kv_all.shape[2]
    row = pl.BlockSpec((1, XA_ROWS, D_MODEL), lambda bi, i: (bi, i, 0))
    mixer_specs, mixer_args = [], []
    if mixer is not None:
        a, w_mix = mixer
        K = a.shape[-1]
        mixer_specs = [pl.BlockSpec((1, XA_ROWS, K), lambda bi, i: (bi, i, 0)),
                       _resident((K, D_MODEL)),
                       _resident((1, D_MODEL), (layer, 1)), _resident((1, D_MODEL), (layer, 1))]
        mixer_args = [a, w_mix, ln_g, ln_b]
    return pl.pallas_call(
        functools.partial(_xattn_ln_kernel, mixer_proj=mixer is not None),
        grid=(B, S // XA_ROWS),
        in_specs=[row] + mixer_specs + [
            pl.BlockSpec((None, None, M, 2 * D_MODEL), lambda bi, i: (layer, bi, 0, 0)),
            _resident((D_MODEL, D_MODEL)), _resident((D_MODEL, D_MODEL)),
            _resident((1, D_MODEL), (layer, 2)), _resident((1, D_MODEL), (layer, 2))],
        out_specs=row,
        out_shape=jax.ShapeDtypeStruct((B, S, D_MODEL), F32),
        compiler_params=_params(2),
        name="xattn_ln",
    )(x, *mixer_args, kv_all, wq, wo, ln_g, ln_b)


DA_PROJ_ROWS = 512
DA_QBLOCK = 512
DA_WIDTH = DA_HEADS * DA_V_DIM


def _da_proj_kernel(x_ref, w_ref, cos_ref, sin_ref, q_ref, k_ref, v_ref):
    xb = x_ref[0].astype(BF16)
    cos = cos_ref[0]
    sin = sin_ref[0]
    lane = lax.broadcasted_iota(jnp.int32, (1, DA_V_DIM), 1)
    first_half = (lane % DA_HEAD_DIM) < (DA_HEAD_DIM // 2)
    q_scale = DA_HEAD_DIM ** -0.5 * LOG2E
    for part, out_ref, scale in ((0, q_ref, q_scale), (1, k_ref, 1.0)):
        y = _dot(xb, w_ref[:, part * DA_WIDTH:(part + 1) * DA_WIDTH])
        for h in range(DA_HEADS):
            yh = y[:, h * DA_V_DIM:(h + 1) * DA_V_DIM]
            partner = jnp.where(first_half,
                                pltpu.roll(yh, DA_V_DIM - DA_HEAD_DIM // 2, axis=1),
                                pltpu.roll(yh, DA_HEAD_DIM // 2, axis=1))
            r = yh * cos + partner * sin
            out_ref[h] = (r * scale).astype(BF16)
    v = _dot(xb, w_ref[:, 2 * DA_WIDTH:]).astype(BF16)
    for h in range(DA_HEADS):
        v_ref[h] = v[:, h * DA_V_DIM:(h + 1) * DA_V_DIM]


def _da_proj(x, w_qkv, cos_t, sin_t):
    B, S, _ = x.shape
    row = pl.BlockSpec((1, DA_PROJ_ROWS, D_MODEL), lambda bi, i: (bi, i, 0))
    tab = pl.BlockSpec((1, DA_PROJ_ROWS, LANES), lambda bi, i: (bi, i, 0))
    heads = pl.BlockSpec((None, DA_HEADS, DA_PROJ_ROWS, DA_V_DIM), lambda bi, i: (bi, 0, i, 0))
    out = jax.ShapeDtypeStruct((B, DA_HEADS, S, DA_V_DIM), BF16)
    return pl.pallas_call(
        _da_proj_kernel,
        grid=(B, S // DA_PROJ_ROWS),
        in_specs=[row, _resident((D_MODEL, 3 * DA_WIDTH)), tab, tab],
        out_specs=[heads, heads, heads],
        out_shape=[out, out, out],
        compiler_params=_params(2),
        name="da_proj",
    )(x, w_qkv, cos_t, sin_t)


DA_ONES_ROWS = 16


DA_HEADS_PER_STEP = 4


def _da_attn_kernel(q_ref, k_ref, v_ref, lq_ref, lk_ref, sg_ref, o_ref, vt_ref, *, lam_init):
    G = DA_HEADS_PER_STEP
    S = q_ref.shape[1]
    tq = DA_QBLOCK
    for g in range(G):
        vt_ref[g, :DA_V_DIM, :] = v_ref[g].astype(F32).T.astype(BF16)
        vt_ref[g, DA_V_DIM:, :] = jnp.ones((DA_ONES_ROWS, S), BF16)
    lqk = lq_ref[...] * lk_ref[...]
    lam = (jnp.exp(jnp.sum(lqk[0:1], axis=-1, keepdims=True))
           - jnp.exp(jnp.sum(lqk[1:2], axis=-1, keepdims=True)) + lam_init)
    lane = lax.broadcasted_iota(jnp.int32, (1, DA_V_DIM), 1)
    map0 = lane < DA_HEAD_DIM
    sub_g = sg_ref[...] * (1.0 - lam_init)
    th = tq // 2
    key = lax.broadcasted_iota(jnp.int32, (th, 2 * th), 0)
    qry = lax.broadcasted_iota(jnp.int32, (th, 2 * th), 1)
    tri = key <= jnp.where(qry >= th, qry - th, qry)
    neg = jnp.finfo(F32).min

    def both_maps(q):
        zero = jnp.zeros_like(q)
        return jnp.concatenate([jnp.where(map0, q, zero), jnp.where(map0, zero, q)], axis=0)

    def scores(g, i):
        n0 = i * tq
        qa = both_maps(q_ref[g, n0:n0 + th, :])
        qb = both_maps(q_ref[g, n0 + th:n0 + tq, :])
        ka = k_ref[g, n0:n0 + th, :]
        kb = k_ref[g, n0 + th:n0 + tq, :]
        s_aa = jnp.where(tri, _dot_nt(ka, qa), neg)
        s_ab = _dot_nt(ka, qb)
        s_bb = jnp.where(tri, _dot_nt(kb, qb), neg)
        s_pa = _dot_nt(k_ref[g, :n0, :], qa) if i > 0 else None
        s_pb = _dot_nt(k_ref[g, :n0, :], qb) if i > 0 else None
        return s_pa, s_pb, s_aa, s_ab, s_bb

    def finish(g, i, s_pa, s_pb, s_aa, s_ab, s_bb):
        n0 = i * tq
        m_a = jnp.max(s_aa, axis=0, keepdims=True)
        m_b = jnp.maximum(jnp.max(s_ab, axis=0, keepdims=True), jnp.max(s_bb, axis=0, keepdims=True))
        if i > 0:
            m_a = jnp.maximum(m_a, jnp.max(s_pa, axis=0, keepdims=True))
            m_b = jnp.maximum(m_b, jnp.max(s_pb, axis=0, keepdims=True))
        acc_a = _dot(vt_ref[g, :, n0:n0 + th], jnp.exp2(s_aa - m_a).astype(BF16))
        acc_b = (_dot(vt_ref[g, :, n0:n0 + th], jnp.exp2(s_ab - m_b).astype(BF16))
                 + _dot(vt_ref[g, :, n0 + th:n0 + tq], jnp.exp2(s_bb - m_b).astype(BF16)))
        if i > 0:
            acc_a = acc_a + _dot(vt_ref[g, :, :n0], jnp.exp2(s_pa - m_a).astype(BF16))
            acc_b = acc_b + _dot(vt_ref[g, :, :n0], jnp.exp2(s_pb - m_b).astype(BF16))
        for half, acc in ((0, acc_a), (1, acc_b)):
            r = acc[:DA_V_DIM] / acc[DA_V_DIM:DA_V_DIM + 1]
            o = r[:, :th] - lam * r[:, th:]
            o = o * lax.rsqrt(jnp.mean(o * o, axis=0, keepdims=True) + LN_EPS) * sub_g
            r0 = n0 + half * th
            o_ref[0, r0:r0 + th, g * DA_V_DIM:(g + 1) * DA_V_DIM] = o.T.astype(BF16)

    items = [(g, i) for i in range(S // tq) for g in range(G)]
    pending = scores(*items[0])
    for n, (g, i) in enumerate(items):
        upcoming = scores(*items[n + 1]) if n + 1 < len(items) else None
        finish(g, i, *pending)
        pending = upcoming


def _da_attn(q, k, v, lam_q, lam_k, subln_g, j, lam_init):
    B, _, S, _ = q.shape
    G = DA_HEADS_PER_STEP
    head = pl.BlockSpec((None, G, S, DA_V_DIM), lambda bi, h: (bi, h, 0, 0))
    return pl.pallas_call(
        functools.partial(_da_attn_kernel, lam_init=lam_init),
        grid=(B, DA_HEADS // G),
        in_specs=[head, head, head, _resident((2, DA_HEAD_DIM), (j,)),
                  _resident((2, DA_HEAD_DIM), (j,)), _resident((DA_V_DIM, 1), (j,))],
        out_specs=pl.BlockSpec((1, S, G * DA_V_DIM), lambda bi, h: (bi, 0, h)),
        out_shape=jax.ShapeDtypeStruct((B, S, DA_WIDTH), BF16),
        scratch_shapes=[pltpu.VMEM((G, DA_V_DIM + DA_ONES_ROWS, S), BF16)],
        compiler_params=_params(2),
        name="da_attn",
    )(q, k, v, lam_q, lam_k, subln_g)


POOL_ROWS = 512
POOL_SKIP = 8
assert POOL_WINDOWS == tuple(2 ** (l + 1) for l in range(len(POOL_WINDOWS)))
assert POOL_HALO >= POOL_SKIP + sum(w // 2 for w in POOL_WINDOWS) and POOL_HALO % 8 == 0


def _pool_ln_kernel(x_ref, prev_ref, w_ref, bias_ref, scale_ref, g_ref, b_ref, o_ref, *bufs,
                    tiles_per_seq):
    i = pl.program_id(0)
    t0 = (i % tiles_per_seq) * POOL_ROWS
    x = x_ref[...]
    first = (i % tiles_per_seq) == 0
    n_rows = POOL_HALO + POOL_ROWS
    bufs[0][0:POOL_HALO, :] = jnp.where(first, 0.0, prev_ref[...])
    bufs[0][POOL_HALO:, :] = x
    t = t0 + lax.broadcasted_iota(jnp.int32, (POOL_ROWS, 1), 0)
    outs = []
    for level, w in enumerate(POOL_WINDOWS):
        src = bufs[level]
        shift = w // 2
        tot = src[POOL_SKIP:n_rows, :] + src[POOL_SKIP - shift:n_rows - shift, :]
        if level + 1 < len(POOL_WINDOWS):
            bufs[level + 1][0:POOL_SKIP, :] = jnp.zeros((POOL_SKIP, tot.shape[1] - POOL_GROUP_DIM), F32)
            bufs[level + 1][POOL_SKIP:, :] = tot[:, POOL_GROUP_DIM:]
        xg = x[:, level * POOL_GROUP_DIM:(level + 1) * POOL_GROUP_DIM]
        cnt = jnp.minimum(t + 1, w).astype(F32)
        pooled = tot[POOL_HALO - POOL_SKIP:, 0:POOL_GROUP_DIM] / cnt - xg
        outs.append(_dot(pooled.astype(BF16), w_ref[level]))
    y = (jnp.concatenate(outs, axis=-1) + bias_ref[...]) * scale_ref[...]
    o_ref[...] = _layer_norm(ALPHA * x + y, g_ref[...], b_ref[...])


def _pool_ln(x2d, seq_len, w, bias_all, scale_all, j, ln_g, ln_b, layer):
    T = x2d.shape[0]
    tiles_per_seq = seq_len // POOL_ROWS
    halo_blocks = POOL_ROWS // POOL_HALO
    return pl.pallas_call(
        functools.partial(_pool_ln_kernel, tiles_per_seq=tiles_per_seq),
        grid=(T // POOL_ROWS,),
        in_specs=[pl.BlockSpec((POOL_ROWS, D_MODEL), lambda i: (i, 0)),
                  pl.BlockSpec((POOL_HALO, D_MODEL),
                               lambda i: (jnp.maximum(i * halo_blocks - 1, 0), 0)),
                  _resident(w.shape),
                  _resident((1, D_MODEL), (j,)), _resident((1, D_MODEL), (j,)),
                  _resident((1, D_MODEL), (layer, 1)), _resident((1, D_MODEL), (layer, 1))],
        out_specs=pl.BlockSpec((POOL_ROWS, D_MODEL), lambda i: (i, 0)),
        out_shape=jax.ShapeDtypeStruct((T, D_MODEL), F32),
        scratch_shapes=[pltpu.VMEM((POOL_HALO + POOL_ROWS, D_MODEL - l * POOL_GROUP_DIM), F32)
                        for l in range(len(POOL_WINDOWS))],
        compiler_params=_params(1),
        name="pool_ln",
    )(x2d, x2d, w, bias_all, scale_all, ln_g, ln_b)


RET_PROJ_ROWS = 512
RET_QK_WIDTH = RET_HEADS * RET_QK_DIM
RET_V_WIDTH = RET_HEADS * RET_V_DIM
RET_STEP_ROWS = 512


def _ret_proj_kernel(x_ref, w_ref, cos_ref, sin_ref, q_ref, k_ref, v_ref, sg_ref):
    xb = x_ref[0].astype(BF16)
    cos = cos_ref[0]
    sin = sin_ref[0]
    half = RET_QK_DIM // 2
    for part, out_ref, scale in ((0, q_ref, 1.0), (1, k_ref, RET_QK_DIM ** -0.5)):
        y = _dot(xb, w_ref[:, part * RET_QK_WIDTH:(part + 1) * RET_QK_WIDTH])
        for h in range(RET_HEADS):
            x1 = y[:, h * RET_QK_DIM:h * RET_QK_DIM + half]
            x2 = y[:, h * RET_QK_DIM + half:(h + 1) * RET_QK_DIM]
            out_ref[0, :, h * RET_QK_DIM:h * RET_QK_DIM + half] = (
                (x1 * cos - x2 * sin) * scale).astype(BF16)
            out_ref[0, :, h * RET_QK_DIM + half:(h + 1) * RET_QK_DIM] = (
                (x2 * cos + x1 * sin) * scale).astype(BF16)
    v0 = 2 * RET_QK_WIDTH
    g0 = v0 + RET_V_WIDTH
    for c in range(0, RET_V_WIDTH, 1024):
        v_ref[0, :, c:c + 1024] = _dot(xb, w_ref[:, v0 + c:v0 + c + 1024]).astype(BF16)
        gate = _dot(xb, w_ref[:, g0 + c:g0 + c + 1024])
        sg_ref[0, :, c:c + 1024] = (gate * jax.nn.sigmoid(gate)).astype(BF16)


def _ret_proj(x, w, cos_t, sin_t):
    B, S, _ = x.shape
    grid_row = lambda width: pl.BlockSpec((1, RET_PROJ_ROWS, width), lambda bi, i: (bi, i, 0))
    qk = jax.ShapeDtypeStruct((B, S, RET_QK_WIDTH), BF16)
    vg = jax.ShapeDtypeStruct((B, S, RET_V_WIDTH), BF16)
    return pl.pallas_call(
        _ret_proj_kernel,
        grid=(B, S // RET_PROJ_ROWS),
        in_specs=[grid_row(D_MODEL), _resident(w.shape),
                  grid_row(LANES), grid_row(LANES)],
        out_specs=[grid_row(RET_QK_WIDTH), grid_row(RET_QK_WIDTH),
                   grid_row(RET_V_WIDTH), grid_row(RET_V_WIDTH)],
        out_shape=[qk, qk, vg, vg],
        compiler_params=_params(2),
        name="ret_proj",
    )(x, w, cos_t, sin_t)


def _ret_core_kernel(q_ref, k_ref, v_ref, sg_ref, o_ref, state_ref):
    C = RET_CHUNK

    @pl.when(pl.program_id(1) == 0)
    def _():
        state_ref[...] = jnp.zeros_like(state_ref)

    ri = lax.broadcasted_iota(jnp.int32, (C, C), 0)
    ci = lax.broadcasted_iota(jnp.int32, (C, C), 1)
    rel = (ri - ci).astype(F32)
    idx = lax.broadcasted_iota(jnp.int32, (C, 1), 0).astype(F32)
    decays = []
    for h in range(RET_HEADS):
        log_gamma = math.log(1.0 - 2.0 ** (-5.0 - h))
        decays.append((jnp.where(rel >= 0, jnp.exp(jnp.maximum(rel, 0.0) * log_gamma), 0.0),
                       jnp.exp((idx + 1.0) * log_gamma),
                       jnp.exp((C - 1.0 - idx) * log_gamma),
                       math.exp(C * log_gamma)))
    for c in range(RET_STEP_ROWS // C):
        rows = slice(c * C, (c + 1) * C)
        for h in range(RET_HEADS):
            d_intra, q_decay, k_decay, chunk_decay = decays[h]
            qs = slice(h * RET_QK_DIM, (h + 1) * RET_QK_DIM)
            vs = slice(h * RET_V_DIM, (h + 1) * RET_V_DIM)
            qc = q_ref[0, rows, qs]
            kc = k_ref[0, rows, qs]
            vc = v_ref[0, rows, vs]
            state = state_ref[h]
            att = _dot_nt(qc, kc) * d_intra
            inner = _dot(att.astype(BF16), vc)
            cross = _dot(qc, state.astype(BF16)) * q_decay
            kd = (kc.astype(F32) * k_decay).astype(BF16)
            state_ref[h] = state * chunk_decay + _dot_tn(kd, vc)
            o = inner + cross
            mu = jnp.mean(o, axis=-1, keepdims=True)
            d = o - mu
            var = jnp.mean(d * d, axis=-1, keepdims=True)
            on = d * lax.rsqrt(var + LN_EPS)
            o_ref[0, rows, vs] = (sg_ref[0, rows, vs].astype(F32) * on).astype(BF16)


def _ret_core(q, k, v, sg):
    B, S, _ = q.shape
    blk = lambda width: pl.BlockSpec((1, RET_STEP_ROWS, width), lambda bi, i: (bi, i, 0))
    return pl.pallas_call(
        _ret_core_kernel,
        grid=(B, S // RET_STEP_ROWS),
        in_specs=[blk(RET_QK_WIDTH), blk(RET_QK_WIDTH), blk(RET_V_WIDTH), blk(RET_V_WIDTH)],
        out_specs=blk(RET_V_WIDTH),
        out_shape=jax.ShapeDtypeStruct((B, S, RET_V_WIDTH), BF16),
        scratch_shapes=[pltpu.VMEM((RET_HEADS, RET_QK_DIM, RET_V_DIM), F32)],
        compiler_params=_params(2),
        name="ret_core",
    )(q, k, v, sg)


def _rope_constants():
    da_inv = 1.0 / (ROPE_THETA ** (jnp.arange(0, DA_HEAD_DIM, 2, dtype=F32) / DA_HEAD_DIM))
    ret_inv = 1.0 / (ROPE_THETA ** jnp.linspace(0.0, 1.0, RET_QK_DIM // 2, dtype=F32))
    half = DA_HEAD_DIM // 2
    da_sign = jnp.concatenate([-jnp.ones(half, F32), jnp.ones(half, F32)] * (LANES // DA_HEAD_DIM))
    return jnp.stack([jnp.tile(da_inv, LANES // half), da_sign, ret_inv, jnp.ones(LANES, F32)])


def kernel(x, mem, positions, ffn_w_in, ffn_w_out, ln_g, ln_b, da_w_qkv, da_w_o, da_lam_q, da_lam_k, da_subln_g, pool_w, pool_b, pool_scale, ret_w_qkvg, ret_w_o, xa_wq, xa_wkv, xa_wo):
    B, S, D = x.shape
    T = B * S
    pos_lanes = jnp.broadcast_to(positions.astype(F32).reshape(T, 1), (T, LANES))
    kv_all = _mem_kv(mem.reshape(-1, D), xa_wkv).reshape(DEPTH, B, mem.shape[1], 2 * D)
    ln_g = ln_g.reshape(DEPTH, 4, 1, D)
    ln_b = ln_b.reshape(DEPTH, 4, 1, D)
    pool_w = pool_w.reshape(-1, D, POOL_GROUP_DIM)
    pool_b = pool_b.reshape(-1, 1, D)
    pool_scale = pool_scale.reshape(-1, 1, D)
    da_subln_g = da_subln_g.reshape(-1, DA_V_DIM, 1)

    def layer_weights(i):
        m, j = i % N_MIXERS, i // N_MIXERS
        mixer = ([("qkv", da_w_qkv, (j,)), ("o", da_w_o, (j,))] if m == 0 else
                 [("pool", pool_w, (j,))] if m == 1 else
                 [("qkvg", ret_w_qkvg, (j,)), ("o", ret_w_o, (j,))])
        return mixer + [("xa_wq", xa_wq, (i,)), ("xa_wo", xa_wo, (i,)),
                        ("w_in", ffn_w_in, (i, 1)), ("w_out", ffn_w_out, (i, 1))]

    def ffn(x, w, i, ln_idx, items, rope=None):
        y, cast, tables = _ffn_ln(x, w["w_in"], w["w_out"], ln_g, ln_b, i, ln_idx,
                                  [(arr, lead) for _, arr, lead in items], rope)
        return y, {name: c for (name, _, _), c in zip(items, cast)}, tables

    w = {"w_in": ffn_w_in[0, 0].astype(BF16), "w_out": ffn_w_out[0, 0].astype(BF16)}
    x = x.reshape(T, D)
    for i in range(DEPTH):
        m, j = i % N_MIXERS, i // N_MIXERS
        x, w, tables = ffn(x, w, i, 0, layer_weights(i),
                           (pos_lanes, _rope_constants()) if i == 0 else None)
        if i == 0:
            da_cos, da_sin, ret_cos, ret_sin = (t.reshape(B, S, LANES) for t in tables)
        mixer = None
        if m == 0:
            lam_init = 0.8 - 0.6 * math.exp(-0.3 * i)
            q, k, v = _da_proj(x.reshape(B, S, D), w["qkv"], da_cos, da_sin)
            mixer = (_da_attn(q, k, v, da_lam_q, da_lam_k, da_subln_g, j, lam_init), w["o"])
        elif m == 1:
            pool_w_bf16 = w["pool"].reshape(len(POOL_WINDOWS), POOL_GROUP_DIM, POOL_GROUP_DIM)
            x = _pool_ln(x, S, pool_w_bf16, pool_b, pool_scale, j, ln_g, ln_b, i)
        else:
            q, k, v, sg = _ret_proj(x.reshape(B, S, D), w["qkvg"], ret_cos, ret_sin)
            mixer = (_ret_core(q, k, v, sg), w["o"])
        x = _xattn_ln(x.reshape(B, S, D), kv_all, w["xa_wq"], w["xa_wo"], ln_g, ln_b, i,
                      mixer).reshape(T, D)
        nxt = ([("w_in", ffn_w_in, (i + 1, 0)), ("w_out", ffn_w_out, (i + 1, 0))]
               if i + 1 < DEPTH else [])
        x, w, _ = ffn(x, w, i, 3, nxt)
    return x.reshape(B, S, D)
```
